```python
import jax, jax.numpy as jnp
from jax import lax
import numpy as np

D_MODEL = 1024
BATCH = 4
SEQ = 4096
DEPTH = 4
DEC_BATCH = 32
DEC_SEQ = 4
PAST_LEN = 8192
PAGE_SIZE = 128

H_A = 8
HD_A = 64
W_A = H_A * HD_A
Q_BLOCK = 128
SB_BIAS_LO = 5.0
SB_BIAS_HI = 10.0
G_B = 4
CHUNK_B = 128
W_B = 512
C_B = W_B // G_B
H_C = 4
DK_C = 128
DV_C = 128
W_C = H_C * DV_C
RET_CHUNK = 128
ROPE_BASE = 10000.0
D_FF = 4 * D_MODEL
EPS = 1e-6
SPLIT_SIZES = (W_A, W_A, W_A, W_B, W_B, H_C * DK_C, H_C * DK_C, W_C, W_C, D_MODEL, D_MODEL, D_MODEL)
N_IN = W_A * 3 + W_B * 2 + H_C * DK_C * 2 + W_C * 2 + D_MODEL * 3

kernel_name = 'hybrid_sb_gmlp_retention_decoder_step'


def rmsnorm(x, g):
    xf = x.astype(jnp.float32)
    y = xf * lax.rsqrt(jnp.mean(xf * xf, axis=-1, keepdims=True) + EPS)
    return (y * g.astype(jnp.float32)).astype(x.dtype)


def center_norm(x):
    xf = x.astype(jnp.float32)
    mu = jnp.mean(xf, axis=-1, keepdims=True)
    var = jnp.mean(jnp.square(xf - mu), axis=-1, keepdims=True)
    return (xf - mu) * lax.rsqrt(var + EPS)


def rotary(x, pos):
    half = x.shape[-1] // 2
    freqs = ROPE_BASE ** (-jnp.arange(half, dtype=jnp.float32) / half)
    ang = pos.astype(jnp.float32)[:, None] * freqs[None, :]
    cos = jnp.cos(ang)[:, None, :]
    sin = jnp.sin(ang)[:, None, :]
    xf = x.astype(jnp.float32)
    x1, x2 = xf[..., :half], xf[..., half:]
    return jnp.concatenate([x1 * cos - x2 * sin, x1 * sin + x2 * cos], axis=-1).astype(x.dtype)


def adaln(c, w, b):
    mod = jax.nn.silu(c) @ w + b
    return jnp.split(mod[:, None, :], 6, axis=-1)


def sb_attend(q, k, v, q_pos, k_pos, bias):
    z = jnp.einsum('bqhd,bkhd->bhqk', q.astype(jnp.float32), k.astype(jnp.float32)) * (q.shape[-1] ** -0.5)
    z = z + bias.astype(jnp.float32)[None, :, None, None]
    mask = k_pos[None, :] < q_pos[:, None]
    log_keep = jnp.where(mask, jax.nn.log_sigmoid(-z), 0.0)
    after = lax.cumsum(log_keep, axis=3, reverse=True) - log_keep
    a = jnp.where(mask, jnp.exp(jax.nn.log_sigmoid(z) + after), 0.0)
    return jnp.einsum('bhqk,bkhd->bqhd', a.astype(v.dtype), v)


def sb_prompt(q, k, v, bias):
    b_, s_, h_, d_ = q.shape
    nb = s_ // Q_BLOCK
    qb = q.reshape(b_, nb, Q_BLOCK, h_, d_).transpose(1, 0, 2, 3, 4)
    qpos = jnp.arange(s_).reshape(nb, Q_BLOCK)
    kpos = jnp.arange(s_)
    out = lax.map(lambda a: sb_attend(a[0], k, v, a[1], kpos, bias), (qb, qpos))
    return out.transpose(1, 0, 2, 3, 4).reshape(b_, s_, h_, v.shape[-1])


def spatial_gate(u, v, w_s, b_s):
    b_, l_, _ = v.shape
    t_ = min(l_, CHUNK_B)
    vc = v.reshape(b_, l_ // t_, t_, G_B, C_B)
    w = jnp.tril(w_s[:, :t_, :t_])
    mixed = jnp.einsum('gts,bnsgc->bntgc', w, vc) + b_s[:, :t_].T[None, None, :, :, None]
    return u * mixed.reshape(b_, l_, W_B)


def retention_chunk(q, k, v, s_prev, log_gamma):
    l_ = q.shape[1]
    i = jnp.arange(l_, dtype=jnp.float32)
    diff = i[:, None] - i[None, :]
    decay = jnp.where(diff >= 0, jnp.exp(log_gamma[:, None, None] * jnp.maximum(diff, 0.0)), 0.0)
    qf, kf, vf = q.astype(jnp.float32), k.astype(jnp.float32), v.astype(jnp.float32)
    scores = jnp.einsum('blhd,bmhd->bhlm', qf, kf) * decay[None]
    inner = jnp.einsum('bhlm,bmhe->blhe', scores, vf)
    xi = jnp.exp(log_gamma[None, :] * (i[:, None] + 1.0))[None, :, :, None]
    cross = jnp.einsum('blhd,bhde->blhe', qf, s_prev) * xi
    zeta = jnp.exp(log_gamma[None, :] * (l_ - 1.0 - i[:, None]))[None, :, :, None]
    s_new = jnp.exp(log_gamma * l_)[None, :, None, None] * s_prev + jnp.einsum('blhd,blhe->bhde', kf * zeta, vf)
    return inner + cross, s_new


def retention_prompt(q, k, v, log_gamma):
    b_, s_, h_, dk = q.shape
    nc = s_ // RET_CHUNK
    def to_chunks(t):
        return t.reshape(b_, nc, RET_CHUNK, h_, t.shape[-1]).transpose(1, 0, 2, 3, 4)
    s0 = jnp.zeros((b_, h_, dk, v.shape[-1]), jnp.float32)
    def step(st, qkv):
        o, st = retention_chunk(qkv[0], qkv[1], qkv[2], st, log_gamma)
        return st, o
    s_fin, o = lax.scan(step, s0, (to_chunks(q), to_chunks(k), to_chunks(v)))
    return o.transpose(1, 0, 2, 3, 4).reshape(b_, s_, h_, v.shape[-1]), s_fin


def mixer_inputs(x, shift, scale, g_norm, w_in, g_qa, g_ka, g_vb, pos):
    b_, l_, _ = x.shape
    h = rmsnorm(x, g_norm) * (1.0 + scale) + shift
    p = h @ w_in
    parts = []
    start = 0
    for size in SPLIT_SIZES:
        parts.append(p[..., start:start + size])
        start += size
    qa, ka, va, ub, vb, qc, kc, vc, gc, ga, gb, gcm = parts
    qa = rmsnorm(qa.reshape(b_, l_, H_A, HD_A), g_qa)
    ka = rmsnorm(ka.reshape(b_, l_, H_A, HD_A), g_ka)
    va = va.reshape(b_, l_, H_A, HD_A)
    ub = jax.nn.gelu(ub)
    vb = (center_norm(jax.nn.gelu(vb)) * g_vb.astype(jnp.float32)).astype(x.dtype)
    qc = rotary(qc.reshape(b_, l_, H_C, DK_C), pos)
    kc = rotary(kc.reshape(b_, l_, H_C, DK_C), pos) * (DK_C ** -0.5)
    vc = vc.reshape(b_, l_, H_C, DV_C)
    return qa, ka, va, ub, vb, qc, kc, vc, jax.nn.silu(gc), jax.nn.sigmoid(ga), jax.nn.sigmoid(gb), jax.nn.sigmoid(gcm)


def merge_and_mlp(x, gate1, shift2, scale2, gate2, oa, ob, oc, gc, ga, gb, gcm,
                  w_pa, w_pb, w_pc, w_o, g_norm_mlp, w1, w2):
    b_, l_, _ = x.shape
    oc = (center_norm(oc).astype(x.dtype).reshape(b_, l_, W_C)) * gc
    merged = ga * (oa.reshape(b_, l_, W_A) @ w_pa) + gb * (ob @ w_pb) + gcm * (oc @ w_pc)
    x = x + gate1 * (merged @ w_o)
    h2 = rmsnorm(x, g_norm_mlp) * (1.0 + scale2) + shift2
    return x + gate2 * (jnp.square(jax.nn.relu(h2 @ w1)) @ w2)


def setup_inputs(seed: int = 0) -> dict:
    key = jax.random.key(seed)
    ks = jax.random.split(key, 32)
    n_pages = PAST_LEN // PAGE_SIZE
    n_pool = (DEC_BATCH * n_pages * 5) // 4
    f32 = jnp.float32
    def nrm(k, shape, scale):
        return jax.random.normal(k, shape, f32) * scale
    page_table = jax.random.permutation(ks[0], n_pool)[:DEC_BATCH * n_pages].astype(jnp.int32).reshape(DEC_BATCH, n_pages)
    sb_bias = -jnp.linspace(SB_BIAS_LO, SB_BIAS_HI, H_A, dtype=f32)[None, :] + nrm(ks[24], (DEPTH, H_A), 0.1)
    return {
        'x_prompt': nrm(ks[1], (BATCH, SEQ, D_MODEL), 1.0),
        'x_sample': nrm(ks[2], (DEC_BATCH, DEC_SEQ, D_MODEL), 1.0),
        'cache_k': nrm(ks[3], (DEPTH, n_pool, PAGE_SIZE, H_A, HD_A), 1.0),
        'cache_v': nrm(ks[4], (DEPTH, n_pool, PAGE_SIZE, H_A, HD_A), 1.0),
        'state_ret': nrm(ks[5], (DEPTH, DEC_BATCH, H_C, DK_C, DV_C), 0.5),
        'page_table': page_table,
        'c_prompt': nrm(ks[6], (BATCH, D_MODEL), 1.0),
        'c_sample': nrm(ks[7], (DEC_BATCH, D_MODEL), 1.0),
        'w_ada': nrm(ks[8], (DEPTH, D_MODEL, 6 * D_MODEL), 0.5 * D_MODEL ** -0.5),
        'b_ada': nrm(ks[9], (DEPTH, 6 * D_MODEL), 0.02),
        'g_norm_mix': 1.0 + nrm(ks[10], (DEPTH, D_MODEL), 0.02),
        'g_norm_mlp': 1.0 + nrm(ks[11], (DEPTH, D_MODEL), 0.02),
        'w_in': nrm(ks[12], (DEPTH, D_MODEL, N_IN), D_MODEL ** -0.5),
        'g_qa': 1.0 + nrm(ks[13], (DEPTH, HD_A), 0.02),
        'g_ka': 1.0 + nrm(ks[14], (DEPTH, HD_A), 0.02),
        'sb_bias': sb_bias,
        'g_vb': 1.0 + nrm(ks[15], (DEPTH, W_B), 0.02),
        'w_spatial': nrm(ks[16], (DEPTH, G_B, CHUNK_B, CHUNK_B), 0.5 * CHUNK_B ** -0.5),
        'b_spatial': 1.0 + nrm(ks[17], (DEPTH, G_B, CHUNK_B), 0.01),
        'w_proj_a': nrm(ks[18], (DEPTH, W_A, D_MODEL), W_A ** -0.5),
        'w_proj_b': nrm(ks[19], (DEPTH, W_B, D_MODEL), W_B ** -0.5),
        'w_proj_c': nrm(ks[20], (DEPTH, W_C, D_MODEL), W_C ** -0.5),
        'w_out': nrm(ks[21], (DEPTH, D_MODEL, D_MODEL), D_MODEL ** -0.5),
        'w_mlp1': nrm(ks[22], (DEPTH, D_MODEL, D_FF), D_MODEL ** -0.5),
        'w_mlp2': nrm(ks[23], (DEPTH, D_FF, D_MODEL), D_FF ** -0.5),
    }


def reference(x_prompt, x_sample, cache_k, cache_v, state_ret, page_table, c_prompt, c_sample,
              w_ada, b_ada, g_norm_mix, g_norm_mlp, w_in, g_qa, g_ka, sb_bias, g_vb, w_spatial, b_spatial,
              w_proj_a, w_proj_b, w_proj_c, w_out, w_mlp1, w_mlp2):
    s_ = x_prompt.shape[1]
    db, l_, _ = x_sample.shape
    n_pages = page_table.shape[1]
    past = n_pages * PAGE_SIZE
    pos_p = jnp.arange(s_)
    pos_s = past + jnp.arange(l_)
    kpos_s = jnp.concatenate([jnp.arange(past), pos_s])
    log_gamma = jnp.log1p(-jnp.exp2(-5.0 - jnp.arange(H_C, dtype=jnp.float32)))
    xp, xs = x_prompt, x_sample
    kp_l, vp_l, ks_l, vs_l, sp_l, ss_l, gv_l = [], [], [], [], [], [], []
    for l in range(DEPTH):
        sh1, sc1, gt1, sh2, sc2, gt2 = adaln(c_prompt, w_ada[l], b_ada[l])
        qa, ka, va, ub, vb, qc, kc, vc, gc, ga, gb, gcm = mixer_inputs(
            xp, sh1, sc1, g_norm_mix[l], w_in[l], g_qa[l], g_ka[l], g_vb[l], pos_p)
        oa = sb_prompt(qa, ka, va, sb_bias[l])
        ob = spatial_gate(ub, vb, w_spatial[l], b_spatial[l])
        oc, s_fin = retention_prompt(qc, kc, vc, log_gamma)
        xp = merge_and_mlp(xp, gt1, sh2, sc2, gt2, oa, ob, oc, gc, ga, gb, gcm,
                           w_proj_a[l], w_proj_b[l], w_proj_c[l], w_out[l], g_norm_mlp[l], w_mlp1[l], w_mlp2[l])
        kp_l.append(ka)
        vp_l.append(va)
        sp_l.append(s_fin)
        sh1, sc1, gt1, sh2, sc2, gt2 = adaln(c_sample, w_ada[l], b_ada[l])
        qa, ka, va, ub, vb, qc, kc, vc, gc, ga, gb, gcm = mixer_inputs(
            xs, sh1, sc1, g_norm_mix[l], w_in[l], g_qa[l], g_ka[l], g_vb[l], pos_s)
        k_past = cache_k[l][page_table].reshape(db, past, H_A, HD_A)
        v_past = cache_v[l][page_table].reshape(db, past, H_A, HD_A)
        k_all = jnp.concatenate([k_past.astype(ka.dtype), ka], axis=1)
        v_all = jnp.concatenate([v_past.astype(va.dtype), va], axis=1)
        oa = sb_attend(qa, k_all, v_all, pos_s, kpos_s, sb_bias[l])
        ob = spatial_gate(ub, vb, w_spatial[l], b_spatial[l])
        oc, s_new = retention_chunk(qc, kc, vc, state_ret[l].astype(jnp.float32), log_gamma)
        xs = merge_and_mlp(xs, gt1, sh2, sc2, gt2, oa, ob, oc, gc, ga, gb, gcm,
                           w_proj_a[l], w_proj_b[l], w_proj_c[l], w_out[l], g_norm_mlp[l], w_mlp1[l], w_mlp2[l])
        ks_l.append(ka)
        vs_l.append(va)
        ss_l.append(s_new)
        gv_l.append(vb)
    return (xp, xs, jnp.stack(kp_l), jnp.stack(vp_l), jnp.stack(ks_l), jnp.stack(vs_l),
            jnp.stack(sp_l), jnp.stack(ss_l), jnp.stack(gv_l))
```

```python
import functools

import jax
import jax.numpy as jnp
import numpy as np
from jax import lax
from jax.experimental import pallas as pl
from jax.experimental.pallas import tpu as pltpu

F32 = jnp.float32
BF16 = jnp.bfloat16

EPS = 1e-6
ROPE_BASE = 10000.0
RET_CHUNK = 128
LANES = 128
V7X_VMEM_BYTES = 64 * 1024 * 1024
VMEM_LIMIT = 56 * 1024 * 1024

GRP = 512
J_GA, J_GB, J_GCM = 0, 2, 4
J_QA, J_KA, J_VA, J_UB, J_VB, J_QC, J_KC, J_VC, J_GC = 6, 7, 8, 9, 10, 11, 12, 13, 14
N_GRP = 15


def _cparams(sem, vmem=VMEM_LIMIT):
    return pltpu.CompilerParams(dimension_semantics=sem, vmem_limit_bytes=vmem)


def _sigmoid(x):
    return 1.0 / (1.0 + jnp.exp(-x))


def _gelu_tanh(x):
    return 0.5 * x * (1.0 + jnp.tanh(np.sqrt(2.0 / np.pi).astype(np.float32) * (x + 0.044715 * (x * x * x))))


def _nt_dot(a, b):
    return lax.dot_general(a, b, (((1,), (1,)), ((), ())), preferred_element_type=F32)


def _adaln_kernel(c_ref, w_ref, b_ref, o_ref):
    c = c_ref[...]
    s = (c * _sigmoid(c)).astype(BF16)
    o_ref[...] = jnp.dot(s, w_ref[...].astype(BF16), preferred_element_type=F32) + b_ref[...]


def _adaln(c_all, w_ada, b_ada, tn=1536):
    depth, d, n = w_ada.shape
    m = c_all.shape[0]
    return pl.pallas_call(
        _adaln_kernel,
        grid=(depth, n // tn),
        in_specs=[
            pl.BlockSpec((m, d), lambda l, j: (0, 0)),
            pl.BlockSpec((None, d, tn), lambda l, j: (l, 0, j)),
            pl.BlockSpec((None, 1, tn), lambda l, j: (l, 0, j)),
        ],
        out_specs=pl.BlockSpec((None, m, tn), lambda l, j: (l, 0, j)),
        out_shape=jax.ShapeDtypeStruct((depth, m, n), F32),
        compiler_params=_cparams(("parallel", "parallel")),
        name="adaln",
    )(c_all, w_ada, b_ada.reshape(depth, 1, n))


def _inproj_kernel(x_ref, shift_ref, scale_ref, g_ref, w_ref, cos_ref, sin_ref, gq_ref, gk_ref, gvb_ref,
                   gsum_ref, p_ref, k_ref, v_ref, vb_ref, h_scr, *, hd_scale, kc_scale):
    j = pl.program_id(1)

    @pl.when(j == 0)
    def _():
        x = x_ref[...]
        ms = jnp.mean(x * x, axis=-1, keepdims=True)
        y = x * lax.rsqrt(ms + EPS) * g_ref[...]
        h_scr[...] = (y * (1.0 + scale_ref[...]) + shift_ref[...]).astype(BF16)

    acc = jnp.dot(h_scr[...], w_ref[...], preferred_element_type=F32)

    def head_norm(a, g):
        ms = jnp.dot((a * a).astype(BF16), gsum_ref[...], preferred_element_type=F32)
        return a * lax.rsqrt(ms + EPS) * g

    def rotary(a):
        outs = []
        for h in range(GRP // LANES):
            xh = a[:, h * LANES:(h + 1) * LANES]
            outs.append(xh * cos_ref[...] + pltpu.roll(xh, LANES // 2, axis=1) * sin_ref[...])
        return jnp.concatenate(outs, axis=1)

    @pl.when(j < J_QA)
    def _():
        p_ref[...] = _sigmoid(acc).astype(p_ref.dtype)

    @pl.when(j == J_QA)
    def _():
        p_ref[...] = (head_norm(acc, gq_ref[...]) * hd_scale).astype(p_ref.dtype)

    @pl.when(j == J_KA)
    def _():
        kn = head_norm(acc, gk_ref[...])
        k_ref[...] = kn
        p_ref[...] = kn.astype(p_ref.dtype)

    @pl.when(j == J_VA)
    def _():
        v_ref[...] = acc
        p_ref[...] = acc.astype(p_ref.dtype)

    @pl.when(j == J_UB)
    def _():
        p_ref[...] = _gelu_tanh(acc).astype(p_ref.dtype)

    @pl.when(j == J_VB)
    def _():
        gl = _gelu_tanh(acc)
        mu = jnp.mean(gl, axis=-1, keepdims=True)
        dl = gl - mu
        var = jnp.mean(dl * dl, axis=-1, keepdims=True)
        vb = dl * lax.rsqrt(var + EPS) * gvb_ref[...]
        vb_ref[...] = vb
        p_ref[...] = vb.astype(p_ref.dtype)

    @pl.when(j == J_QC)
    def _():
        p_ref[...] = rotary(acc).astype(p_ref.dtype)

    @pl.when(j == J_KC)
    def _():
        p_ref[...] = (rotary(acc) * kc_scale).astype(p_ref.dtype)

    @pl.when(j == J_VC)
    def _():
        p_ref[...] = acc.astype(p_ref.dtype)

    @pl.when(j == J_GC)
    def _():
        p_ref[...] = (acc * _sigmoid(acc)).astype(p_ref.dtype)


def _inproj(x2d, mod4, layer, g_norm, w_in_b, cos_t, sin_t, g_qa, g_ka, g_vb, gsum, *, tm, rows_per_mod,
            p_dtype, hd, dk):
    m, d = x2d.shape
    n_in = w_in_b.shape[-1]
    assert n_in == N_GRP * GRP and m % tm == 0
    rm = mod4.shape[2]
    n_pos_blocks = cos_t.shape[0] // tm
    mod_idx = (lambda chunk: (lambda i, j: (layer, i // rows_per_mod, 0, chunk)))
    kern = functools.partial(_inproj_kernel, hd_scale=float(hd) ** -0.5, kc_scale=float(dk) ** -0.5)
    row_blk = lambda i, j: (i, 0)
    return pl.pallas_call(
        kern,
        grid=(m // tm, N_GRP),
        in_specs=[
            pl.BlockSpec((tm, d), row_blk),
            pl.BlockSpec((None, None, rm, d), mod_idx(0)),
            pl.BlockSpec((None, None, rm, d), mod_idx(1)),
            pl.BlockSpec((None, 1, d), lambda i, j: (layer, 0, 0)),
            pl.BlockSpec((None, d, GRP), lambda i, j: (layer, 0, j)),
            pl.BlockSpec((tm, LANES), lambda i, j: (i % n_pos_blocks, 0)),
            pl.BlockSpec((tm, LANES), lambda i, j: (i % n_pos_blocks, 0)),
            pl.BlockSpec((None, 1, GRP), lambda i, j: (layer, 0, 0)),
            pl.BlockSpec((None, 1, GRP), lambda i, j: (layer, 0, 0)),
            pl.BlockSpec((None, 1, GRP), lambda i, j: (layer, 0, 0)),
            pl.BlockSpec((GRP, GRP), lambda i, j: (0, 0)),
        ],
        out_specs=[
            pl.BlockSpec((tm, GRP), lambda i, j: (i, j)),
            pl.BlockSpec((tm, GRP), row_blk),
            pl.BlockSpec((tm, GRP), row_blk),
            pl.BlockSpec((tm, GRP), row_blk),
        ],
        out_shape=[
            jax.ShapeDtypeStruct((m, n_in), p_dtype),
            jax.ShapeDtypeStruct((m, GRP), F32),
            jax.ShapeDtypeStruct((m, GRP), F32),
            jax.ShapeDtypeStruct((m, GRP), F32),
        ],
        scratch_shapes=[pltpu.VMEM((tm, d), BF16)],
        compiler_params=_cparams(("parallel", "arbitrary")),
        name="inproj",
    )(x2d, mod4, mod4, g_norm, w_in_b, cos_t, sin_t, g_qa, g_ka, g_vb, gsum)


def _sb_block(qh, k128, v128, bias, u, c, acc, causal):
    z = _nt_dot(qh, k128) + bias
    lk = -(jnp.maximum(z, 0.0) + jnp.log(1.0 + jnp.exp(-jnp.abs(z))))
    if causal is not None:
        lk = jnp.where(causal, lk, 0.0)
    after = jnp.dot(lk.astype(BF16), u, preferred_element_type=F32) + c
    a = jnp.exp(z + lk + after)
    if causal is not None:
        a = jnp.where(causal, a, 0.0)
    acc = acc + jnp.dot(a.astype(BF16), v128, preferred_element_type=F32)
    c = c + jnp.sum(lk, axis=-1, keepdims=True)
    return c, acc


def _sb_attn_kernel(bias_ref, q_ref, k_ref, v_ref, u_ref, o_ref, *, tq, n_heads, hd):
    i = pl.program_id(1)
    heads_per_vreg = LANES // hd
    lane = lax.broadcasted_iota(jnp.int32, (1, LANES), 1)
    row = lax.broadcasted_iota(jnp.int32, (tq, tq), 0)
    col = lax.broadcasted_iota(jnp.int32, (tq, tq), 1)
    causal = col < row
    u = u_ref[...]
    outs = []
    for pair in range(n_heads // heads_per_vreg):
        lanes = slice(pair * LANES, (pair + 1) * LANES)
        q128 = q_ref[:, lanes]
        out128 = jnp.zeros((tq, LANES), F32)
        for sub in range(heads_per_vreg):
            h = pair * heads_per_vreg + sub
            in_head = (lane >= sub * hd) & (lane < (sub + 1) * hd)
            qh = jnp.where(in_head, q128, jnp.zeros_like(q128))
            bias = bias_ref[h]

            def load_kv(kb):
                start = pl.multiple_of(kb * tq, tq)
                return k_ref[pl.ds(start, tq), lanes], v_ref[pl.ds(start, tq), lanes]

            k128, v128 = load_kv(i)
            c, acc = _sb_block(qh, k128, v128, bias, u, jnp.zeros((tq, 1), F32), jnp.zeros((tq, LANES), F32), causal)

            def body(n, carry):
                k128, v128 = load_kv(i - 1 - n)
                return _sb_block(qh, k128, v128, bias, u, carry[0], carry[1], None)

            c, acc = lax.fori_loop(0, i, body, (c, acc))
            out128 = jnp.where(in_head, acc, out128)
        outs.append(out128)
    o_ref[...] = jnp.concatenate(outs, axis=1).astype(o_ref.dtype)


def _sb_attn(p, sb_bias_l, u, *, batch, seq, tq, n_heads, hd):
    nq = seq // tq
    width = n_heads * hd
    assert width == GRP
    kern = functools.partial(_sb_attn_kernel, tq=tq, n_heads=n_heads, hd=hd)
    return pl.pallas_call(
        kern,
        grid=(batch, nq),
        in_specs=[
            pl.BlockSpec(memory_space=pltpu.SMEM),
            pl.BlockSpec((tq, GRP), lambda b, i: (b * nq + i, J_QA)),
            pl.BlockSpec((seq, GRP), lambda b, i: (b, J_KA)),
            pl.BlockSpec((seq, GRP), lambda b, i: (b, J_VA)),
            pl.BlockSpec((tq, tq), lambda b, i: (0, 0)),
        ],
        out_specs=pl.BlockSpec((tq, GRP), lambda b, i: (b * nq + i, 0)),
        out_shape=jax.ShapeDtypeStruct((batch * seq, GRP), BF16),
        compiler_params=_cparams(("parallel", "arbitrary")),
        name="sb_attn",
    )(sb_bias_l, p, p, p, u)


def _paged_attn_kernel(pt_ref, qbd_ref, bias_ref, knew_ref, vnew_ref, u_ref, ck_hbm, cv_hbm, o_ref,
                       kbuf, vbuf, sem, *, layer, n_groups, g_pages, page, n_heads, hd, n_tok):
    b = pl.program_id(0)
    nb = pl.num_programs(0)
    n_pages = n_groups * g_pages
    gk = g_pages * page
    rows = n_tok * n_heads

    def copies(bb, gi, slot):
        out = []
        for pi in range(g_pages):
            pg = pt_ref[bb, n_pages - (gi + 1) * g_pages + pi]
            out.append(pltpu.make_async_copy(ck_hbm.at[layer, pg], kbuf.at[slot, pi], sem.at[0, slot]))
            out.append(pltpu.make_async_copy(cv_hbm.at[layer, pg], vbuf.at[slot, pi], sem.at[1, slot]))
        return out

    @pl.when(b == 0)
    def _():
        for cp in copies(0, 0, 0):
            cp.start()

    qbd = qbd_ref[...]
    bias = bias_ref[...]
    u = u_ref[...]

    def block(kb, vb, uu, c, acc, mask):
        z = _nt_dot(qbd, kb) + bias
        lk = -(jnp.maximum(z, 0.0) + jnp.log(1.0 + jnp.exp(-jnp.abs(z))))
        if mask is not None:
            lk = jnp.where(mask, lk, 0.0)
        after = jnp.dot(lk.astype(BF16), uu, preferred_element_type=F32) + c
        a = jnp.exp(z + lk + after)
        if mask is not None:
            a = jnp.where(mask, a, 0.0)
        acc = acc + jnp.dot(a.astype(BF16), vb, preferred_element_type=F32)
        return c + jnp.sum(lk, axis=-1, keepdims=True), acc

    n_new = knew_ref.shape[0]
    r = lax.broadcasted_iota(jnp.int32, (rows, n_new), 0) // n_heads
    j = lax.broadcasted_iota(jnp.int32, (rows, n_new), 1)
    c, acc = block(knew_ref[...], vnew_ref[...], u_ref[:n_new, :n_new], jnp.zeros((rows, 1), F32),
                   jnp.zeros((rows, n_heads * hd), F32), j < r)

    def body(gi, carry):
        n = b * n_groups + gi
        slot = n % 2
        last = gi == n_groups - 1

        @pl.when(jnp.logical_not(last))
        def _():
            for cp in copies(b, gi + 1, 1 - slot):
                cp.start()

        @pl.when(last & (b + 1 < nb))
        def _():
            for cp in copies(b + 1, 0, 1 - slot):
                cp.start()

        for cp in copies(b, gi, slot):
            cp.wait()
        kb = kbuf[slot].reshape(gk, n_heads * hd).astype(BF16)
        vb = vbuf[slot].reshape(gk, n_heads * hd).astype(BF16)
        return block(kb, vb, u, carry[0], carry[1], None)

    c, acc = lax.fori_loop(0, n_groups, body, (c, acc))

    rr = lax.broadcasted_iota(jnp.int32, acc.shape, 0) % n_heads
    cc = lax.broadcasted_iota(jnp.int32, acc.shape, 1) // hd
    own = jnp.where(rr == cc, acc, 0.0)
    o_ref[...] = jnp.concatenate(
        [jnp.sum(own[t * n_heads:(t + 1) * n_heads], axis=0, keepdims=True) for t in range(n_tok)], axis=0)


def _paged_attn(page_table, qbd, bias_col, knew, vnew, u, cache_k4, cache_v4, *, layer, g_pages, n_heads, hd, n_tok):
    nb, n_pages = page_table.shape
    page = cache_k4.shape[2]
    width = n_heads * hd
    rows = n_tok * n_heads
    gk = g_pages * page
    n_new = knew.shape[1]
    assert n_pages % g_pages == 0
    kern = functools.partial(_paged_attn_kernel, layer=layer, n_groups=n_pages // g_pages, g_pages=g_pages,
                             page=page, n_heads=n_heads, hd=hd, n_tok=n_tok)
    grid_spec = pltpu.PrefetchScalarGridSpec(
        num_scalar_prefetch=1,
        grid=(nb,),
        in_specs=[
            pl.BlockSpec((None, rows, width), lambda b, pt: (b, 0, 0)),
            pl.BlockSpec((rows, 1), lambda b, pt: (0, 0)),
            pl.BlockSpec((None, n_new, width), lambda b, pt: (b, 0, 0)),
            pl.BlockSpec((None, n_new, width), lambda b, pt: (b, 0, 0)),
            pl.BlockSpec((gk, gk), lambda b, pt: (0, 0)),
            pl.BlockSpec(memory_space=pl.ANY),
            pl.BlockSpec(memory_space=pl.ANY),
        ],
        out_specs=pl.BlockSpec((None, n_tok, width), lambda b, pt: (b, 0, 0)),
        scratch_shapes=[
            pltpu.VMEM((2, g_pages, page, width), F32),
            pltpu.VMEM((2, g_pages, page, width), F32),
            pltpu.SemaphoreType.DMA((2, 2)),
        ],
    )
    return pl.pallas_call(
        kern,
        grid_spec=grid_spec,
        out_shape=jax.ShapeDtypeStruct((nb, n_tok, width), F32),
        compiler_params=_cparams(("arbitrary",)),
        name="paged_attn",
    )(page_table, qbd, bias_col, knew, vnew, u, cache_k4, cache_v4)


def _center_norm(o):
    mu = jnp.mean(o, axis=-1, keepdims=True)
    d = o - mu
    var = jnp.mean(d * d, axis=-1, keepdims=True)
    return d * lax.rsqrt(var + EPS)


def _mix_prompt_kernel(gpow_ref, ub_ref, vb_ref, qc_ref, kc_ref, vc_ref, gc_ref, wsp_ref, bsp_ref, decay_ref,
                       xi_ref, zeta_ref, ob_ref, oc_ref, sfin_ref, state, *, batch, n_grp, n_heads, t):
    ci = pl.program_id(0)

    @pl.when(ci == 0)
    def _():
        state[...] = jnp.zeros_like(state)

    r = lax.broadcasted_iota(jnp.int32, (t, t), 0)
    s = lax.broadcasted_iota(jnp.int32, (t, t), 1)
    tril = s <= r
    for b in range(batch):
        for g in range(n_grp):
            lanes = slice(g * LANES, (g + 1) * LANES)
            w = jnp.where(tril, wsp_ref[g], 0.0).astype(BF16)
            mixed = jnp.dot(w, vb_ref[b, :, lanes], preferred_element_type=F32) + bsp_ref[g]
            ob_ref[b, :, lanes] = (ub_ref[b, :, lanes].astype(F32) * mixed).astype(ob_ref.dtype)
        for h in range(n_heads):
            lanes = slice(h * LANES, (h + 1) * LANES)
            q = qc_ref[b, :, lanes]
            k = kc_ref[b, :, lanes]
            v = vc_ref[b, :, lanes]
            st = state[b, h]
            scores = _nt_dot(q, k) * decay_ref[h]
            inner = jnp.dot(scores.astype(BF16), v, preferred_element_type=F32)
            cross = jnp.dot(q, st.astype(BF16), preferred_element_type=F32) * xi_ref[h]
            kz = (k.astype(F32) * zeta_ref[h]).astype(BF16)
            upd = lax.dot_general(kz, v, (((0,), (0,)), ((), ())), preferred_element_type=F32)
            state[b, h] = gpow_ref[h] * st + upd
            on = _center_norm(inner + cross)
            oc_ref[b, :, lanes] = (on * gc_ref[b, :, lanes].astype(F32)).astype(oc_ref.dtype)

    @pl.when(ci == pl.num_programs(0) - 1)
    def _():
        sfin_ref[...] = state[...]


def _mix_prompt(p3, gpow, wsp_l, bsp_b, decay, xi_b, zeta_b, *, n_grp, n_heads):
    batch, seq, _ = p3.shape
    t = RET_CHUNK
    kern = functools.partial(_mix_prompt_kernel, batch=batch, n_grp=n_grp, n_heads=n_heads, t=t)
    col = lambda jj: (lambda c: (0, c, jj))
    const3 = lambda c: (0, 0, 0)
    return pl.pallas_call(
        kern,
        grid=(seq // t,),
        in_specs=[
            pl.BlockSpec(memory_space=pltpu.SMEM),
            pl.BlockSpec((batch, t, GRP), col(J_UB)),
            pl.BlockSpec((batch, t, GRP), col(J_VB)),
            pl.BlockSpec((batch, t, GRP), col(J_QC)),
            pl.BlockSpec((batch, t, GRP), col(J_KC)),
            pl.BlockSpec((batch, t, GRP), col(J_VC)),
            pl.BlockSpec((batch, t, GRP), col(J_GC)),
            pl.BlockSpec((n_grp, t, t), const3),
            pl.BlockSpec((n_grp, t, LANES), const3),
            pl.BlockSpec((n_heads, t, t), const3),
            pl.BlockSpec((n_heads, t, LANES), const3),
            pl.BlockSpec((n_heads, t, LANES), const3),
        ],
        out_specs=[
            pl.BlockSpec((batch, t, GRP), lambda c: (0, c, 0)),
            pl.BlockSpec((batch, t, GRP), lambda c: (0, c, 0)),
            pl.BlockSpec((batch, n_heads, LANES, LANES), lambda c: (0, 0, 0, 0)),
        ],
        out_shape=[
            jax.ShapeDtypeStruct((batch, seq, GRP), BF16),
            jax.ShapeDtypeStruct((batch, seq, GRP), BF16),
            jax.ShapeDtypeStruct((batch, n_heads, LANES, LANES), F32),
        ],
        scratch_shapes=[pltpu.VMEM((batch, n_heads, LANES, LANES), F32)],
        compiler_params=_cparams(("arbitrary",)),
        name="mix_prompt",
    )(gpow, p3, p3, p3, p3, p3, p3, wsp_l, bsp_b, decay, xi_b, zeta_b)


def _mix_sample_kernel(gpow_ref, ub_ref, vb_ref, qc_ref, kc_ref, vc_ref, gc_ref, mg_ref, bg_ref, dmat_ref,
                       xi_ref, zeta_ref, st_ref, ob_ref, oc_ref, snew_ref, *, n_batch, n_tok):
    h = pl.program_id(0)
    rows = n_batch * n_tok
    mixed = jnp.dot(mg_ref[...].astype(BF16), vb_ref[...].astype(BF16), preferred_element_type=F32) + bg_ref[...]
    ob_ref[...] = ub_ref[...] * mixed

    q = qc_ref[...].astype(BF16)
    k = kc_ref[...]
    v = vc_ref[...].astype(BF16)
    scores = _nt_dot(q, k.astype(BF16)) * dmat_ref[...]
    inner = jnp.dot(scores.astype(BF16), v, preferred_element_type=F32)
    kzt = (k * zeta_ref[...]).T
    gp = gpow_ref[h]
    row_b = lax.broadcasted_iota(jnp.int32, (rows, LANES), 0) // n_tok
    col_b = lax.broadcasted_iota(jnp.int32, (LANES, rows), 1) // n_tok

    def body(b, cross):
        st = st_ref[b]
        cr = jnp.dot(q, st.astype(BF16), preferred_element_type=F32)
        cross = jnp.where(row_b == b, cr, cross)
        upd = jnp.dot(jnp.where(col_b == b, kzt, 0.0).astype(BF16), v, preferred_element_type=F32)
        snew_ref[b] = gp * st + upd
        return cross

    cross = lax.fori_loop(0, n_batch, body, jnp.zeros((rows, LANES), F32))
    on = _center_norm(inner + cross * xi_ref[...])
    oc_ref[...] = on * gc_ref[...]


def _mix_sample(ps, state_ret, layer, gpow4, mg, bg, dmat, xi_s, zeta_s, *, n_batch, n_tok, n_heads):
    rows = n_batch * n_tok
    kern = functools.partial(_mix_sample_kernel, n_batch=n_batch, n_tok=n_tok)
    col = lambda jj: (lambda h: (0, jj * (GRP // LANES) + h))
    per_h = lambda h: (h, 0, 0)
    return pl.pallas_call(
        kern,
        grid=(n_heads,),
        in_specs=[
            pl.BlockSpec(memory_space=pltpu.SMEM),
            pl.BlockSpec((rows, LANES), col(J_UB)),
            pl.BlockSpec((rows, LANES), col(J_VB)),
            pl.BlockSpec((rows, LANES), col(J_QC)),
            pl.BlockSpec((rows, LANES), col(J_KC)),
            pl.BlockSpec((rows, LANES), col(J_VC)),
            pl.BlockSpec((rows, LANES), col(J_GC)),
            pl.BlockSpec((None, rows, rows), per_h),
            pl.BlockSpec((None, rows, LANES), per_h),
            pl.BlockSpec((None, rows, rows), per_h),
            pl.BlockSpec((None, rows, LANES), per_h),
            pl.BlockSpec((None, rows, LANES), per_h),
            pl.BlockSpec((None, n_batch, None, LANES, LANES), lambda h: (layer, 0, h, 0, 0)),
        ],
        out_specs=[
            pl.BlockSpec((rows, LANES), lambda h: (0, h)),
            pl.BlockSpec((rows, LANES), lambda h: (0, h)),
            pl.BlockSpec((n_batch, None, LANES, LANES), lambda h: (0, h, 0, 0)),
        ],
        out_shape=[
            jax.ShapeDtypeStruct((rows, GRP), F32),
            jax.ShapeDtypeStruct((rows, GRP), F32),
            jax.ShapeDtypeStruct((n_batch, n_heads, LANES, LANES), F32),
        ],
        compiler_params=_cparams(("parallel",)),
        name="mix_sample",
    )(gpow4, ps, ps, ps, ps, ps, ps, mg, bg, dmat, xi_s, zeta_s, state_ret)


def _merge_kernel(x_ref, gate_ref, oa_ref, ob_ref, oc_ref, ga_ref, gb_ref, gcm_ref, wa_ref, wb_ref, wc_ref,
                  wo_ref, o_ref):
    def br(o_r, w_r, g_r):
        return g_r[...].astype(F32) * jnp.dot(o_r[...].astype(BF16), w_r[...], preferred_element_type=F32)

    merged = br(oa_ref, wa_ref, ga_ref) + br(ob_ref, wb_ref, gb_ref) + br(oc_ref, wc_ref, gcm_ref)
    y = jnp.dot(merged.astype(BF16), wo_ref[...], preferred_element_type=F32)
    o_ref[...] = x_ref[...] + gate_ref[...] * y


def _merge(x2d, mod4, layer, p, oa, ob, oc, wa, wb, wc, wo, *, tm, rows_per_mod):
    m, d = x2d.shape
    rm = mod4.shape[2]
    row = lambda i: (i, 0)
    wspec = lambda w: pl.BlockSpec((None,) + w.shape[1:], lambda i: (layer, 0, 0))
    return pl.pallas_call(
        _merge_kernel,
        grid=(m // tm,),
        in_specs=[
            pl.BlockSpec((tm, d), row),
            pl.BlockSpec((None, None, rm, d), lambda i: (layer, i // rows_per_mod, 0, 2)),
            pl.BlockSpec((tm, GRP), row),
            pl.BlockSpec((tm, GRP), row),
            pl.BlockSpec((tm, GRP), row),
            pl.BlockSpec((tm, d), lambda i: (i, J_GA // 2)),
            pl.BlockSpec((tm, d), lambda i: (i, J_GB // 2)),
            pl.BlockSpec((tm, d), lambda i: (i, J_GCM // 2)),
            wspec(wa), wspec(wb), wspec(wc), wspec(wo),
        ],
        out_specs=pl.BlockSpec((tm, d), row),
        out_shape=jax.ShapeDtypeStruct((m, d), F32),
        compiler_params=_cparams(("parallel",)),
        name="merge",
    )(x2d, mod4, oa, ob, oc, p, p, p, wa, wb, wc, wo)


def _mlp_kernel(x_ref, shift_ref, scale_ref, gate_ref, g_ref, w1_ref, w2_ref, o_ref, h_scr, acc_scr):
    j = pl.program_id(1)

    @pl.when(j == 0)
    def _():
        x = x_ref[...]
        ms = jnp.mean(x * x, axis=-1, keepdims=True)
        y = x * lax.rsqrt(ms + EPS) * g_ref[...]
        h_scr[...] = (y * (1.0 + scale_ref[...]) + shift_ref[...]).astype(BF16)
        acc_scr[...] = jnp.zeros_like(acc_scr)

    a = jnp.dot(h_scr[...], w1_ref[...], preferred_element_type=F32)
    a = jnp.maximum(a, 0.0)
    acc_scr[...] += jnp.dot((a * a).astype(BF16), w2_ref[...], preferred_element_type=F32)

    @pl.when(j == pl.num_programs(1) - 1)
    def _():
        o_ref[...] = x_ref[...] + gate_ref[...] * acc_scr[...]


def _mlp(x2d, mod4, layer, g_norm, w1, w2, *, tm, tf, rows_per_mod):
    m, d = x2d.shape
    ff = w1.shape[-1]
    rm = mod4.shape[2]
    mod_idx = (lambda chunk: (lambda i, j: (layer, i // rows_per_mod, 0, chunk)))
    return pl.pallas_call(
        _mlp_kernel,
        grid=(m // tm, ff // tf),
        in_specs=[
            pl.BlockSpec((tm, d), lambda i, j: (i, 0)),
            pl.BlockSpec((None, None, rm, d), mod_idx(3)),
            pl.BlockSpec((None, None, rm, d), mod_idx(4)),
            pl.BlockSpec((None, None, rm, d), mod_idx(5)),
            pl.BlockSpec((None, 1, d), lambda i, j: (layer, 0, 0)),
            pl.BlockSpec((None, d, tf), lambda i, j: (layer, 0, j)),
            pl.BlockSpec((None, tf, d), lambda i, j: (layer, j, 0)),
        ],
        out_specs=pl.BlockSpec((tm, d), lambda i, j: (i, 0)),
        out_shape=jax.ShapeDtypeStruct((m, d), F32),
        scratch_shapes=[pltpu.VMEM((tm, d), BF16), pltpu.VMEM((tm, d), F32)],
        compiler_params=_cparams(("parallel", "arbitrary")),
        name="mlp",
    )(x2d, mod4, mod4, mod4, g_norm, w1, w2)


def _rope_tables(pos, dk):
    half = dk // 2
    freqs = ROPE_BASE ** (-jnp.arange(half, dtype=F32) / half)
    ang = pos.astype(F32)[:, None] * freqs[None, :]
    cos, sin = jnp.cos(ang), jnp.sin(ang)
    return jnp.concatenate([cos, cos], axis=1), jnp.concatenate([-sin, sin], axis=1)


def kernel(x_prompt, x_sample, cache_k, cache_v, state_ret, page_table, c_prompt, c_sample, w_ada, b_ada, g_norm_mix, g_norm_mlp, w_in, g_qa, g_ka, sb_bias, g_vb, w_spatial, b_spatial, w_proj_a, w_proj_b, w_proj_c, w_out, w_mlp1, w_mlp2):
    batch, seq, d = x_prompt.shape
    nb, n_tok, _ = x_sample.shape
    depth, n_pool, page, n_heads_a, hd = cache_k.shape
    n_pages = page_table.shape[1]
    past = n_pages * page
    n_grp_b, chunk_b = w_spatial.shape[1], w_spatial.shape[2]
    w_b = g_vb.shape[1]
    n_heads_c, dk, dv = state_ret.shape[2:]
    w_a = n_heads_a * hd
    w_c = n_heads_c * dv
    rows_s = nb * n_tok
    assert w_a == GRP and w_b == GRP and w_c == GRP and n_heads_c * dk == GRP and d == 2 * GRP
    assert dk == LANES and dv == LANES and w_b // n_grp_b == LANES and chunk_b == RET_CHUNK and n_grp_b == n_heads_c
    assert seq % RET_CHUNK == 0 and n_tok <= chunk_b

    sizes = (w_a, w_a, w_a, w_b, w_b, w_c, w_c, w_c, w_c, d, d, d)
    offs = np.concatenate([[0], np.cumsum(sizes)])
    order = (9, 10, 11, 0, 1, 2, 3, 4, 5, 6, 7, 8)
    w_in_b = jnp.concatenate([w_in[:, :, offs[o]:offs[o + 1]] for o in order], axis=-1).astype(BF16)
    wa_b, wb_b, wc_b, wo_b = (w.astype(BF16) for w in (w_proj_a, w_proj_b, w_proj_c, w_out))
    w1_b, w2_b = w_mlp1.astype(BF16), w_mlp2.astype(BF16)
    g_mix = g_norm_mix.reshape(depth, 1, d)
    g_mlp = g_norm_mlp.reshape(depth, 1, d)
    g_qa_t = jnp.tile(g_qa, (1, n_heads_a)).reshape(depth, 1, GRP)
    g_ka_t = jnp.tile(g_ka, (1, n_heads_a)).reshape(depth, 1, GRP)
    g_vb_r = g_vb.reshape(depth, 1, GRP)
    gi = np.arange(GRP) // hd
    gsum = jnp.asarray((gi[:, None] == gi[None, :]).astype(np.float32) / hd, dtype=BF16)

    log_gamma = jnp.log1p(-jnp.exp2(-5.0 - jnp.arange(n_heads_c, dtype=F32)))

    def ret_tables(length):
        i = jnp.arange(length, dtype=F32)
        diff = i[:, None] - i[None, :]
        decay = jnp.where(diff >= 0, jnp.exp(log_gamma[:, None, None] * jnp.maximum(diff, 0.0)), 0.0)
        xi = jnp.exp(log_gamma[:, None] * (i[None, :] + 1.0))
        zeta = jnp.exp(log_gamma[:, None] * (length - 1.0 - i[None, :]))
        return decay, xi, zeta, jnp.exp(log_gamma * length)

    decay_p, xi_p, zeta_p, gpow_p = ret_tables(RET_CHUNK)
    xi_pb = jnp.broadcast_to(xi_p[:, :, None], (n_heads_c, RET_CHUNK, LANES))
    zeta_pb = jnp.broadcast_to(zeta_p[:, :, None], (n_heads_c, RET_CHUNK, LANES))
    decay_s, xi_s, zeta_s, gpow_s = ret_tables(n_tok)
    eye_b = jnp.eye(nb, dtype=F32)
    dmat_s = jnp.stack([jnp.kron(eye_b, decay_s[h]) for h in range(n_heads_c)])
    xi_sb = jnp.broadcast_to(jnp.tile(xi_s, (1, nb))[:, :, None], (n_heads_c, rows_s, LANES))
    zeta_sb = jnp.broadcast_to(jnp.tile(zeta_s, (1, nb))[:, :, None], (n_heads_c, rows_s, LANES))

    bsp_pb = jnp.broadcast_to(b_spatial[:, :, :, None], (depth, n_grp_b, chunk_b, LANES))
    w_sp_s = jnp.tril(w_spatial[:, :, :n_tok, :n_tok])
    mg_s = jnp.einsum("ab,lgts->lgatbs", eye_b, w_sp_s).reshape(depth, n_grp_b, rows_s, rows_s)
    bg_s = jnp.broadcast_to(jnp.tile(b_spatial[:, :, :n_tok], (1, 1, nb))[..., None],
                            (depth, n_grp_b, rows_s, LANES))

    cos_p, sin_p = _rope_tables(jnp.arange(seq), dk)
    cos_s, sin_s = _rope_tables(past + (jnp.arange(rows_s) % n_tok), dk)

    tq = 256
    tri = np.arange(tq)
    u_p = jnp.asarray((tri[:, None] > tri[None, :]).astype(np.float32), dtype=BF16)
    g_pages = 4
    tri = np.arange(g_pages * page)
    u_s = jnp.asarray((tri[:, None] > tri[None, :]).astype(np.float32), dtype=BF16)

    m_all = batch + nb
    m_pad = -(-m_all // 8) * 8
    c_all = jnp.concatenate([c_prompt, c_sample, jnp.zeros((m_pad - m_all, d), F32)], axis=0)
    mod = _adaln(c_all, w_ada, b_ada)
    mod_p = mod[:, :batch].reshape(depth, batch, 1, 6 * d)
    mod_s = jnp.repeat(mod[:, batch:m_all], n_tok, axis=1).reshape(depth, 1, rows_s, 6 * d)

    cache_k4 = cache_k.reshape(depth, n_pool, page, w_a)
    cache_v4 = cache_v.reshape(depth, n_pool, page, w_a)
    bias_col = jnp.tile(sb_bias, (1, n_tok)).reshape(depth, n_tok * n_heads_a, 1)
    row_head = np.arange(n_tok * n_heads_a) % n_heads_a
    qbd_mask = jnp.asarray(row_head[:, None] == (np.arange(w_a) // hd)[None, :])

    tm_p = 512
    xp = x_prompt.reshape(batch * seq, d)
    xs = x_sample.reshape(rows_s, d)
    kp_l, vp_l, ks_l, vs_l, sp_l, ss_l, gv_l = [], [], [], [], [], [], []
    for l in range(depth):
        p, kp, vp, _ = _inproj(xp, mod_p, l, g_mix, w_in_b, cos_p, sin_p, g_qa_t, g_ka_t, g_vb_r, gsum,
                               tm=tm_p, rows_per_mod=seq // tm_p, p_dtype=BF16, hd=hd, dk=dk)
        oa = _sb_attn(p, sb_bias[l], u_p, batch=batch, seq=seq, tq=tq, n_heads=n_heads_a, hd=hd)
        ob, oc, s_fin = _mix_prompt(p.reshape(batch, seq, -1), gpow_p, w_spatial[l], bsp_pb[l], decay_p, xi_pb,
                                    zeta_pb, n_grp=n_grp_b, n_heads=n_heads_c)
        x1 = _merge(xp, mod_p, l, p, oa, ob.reshape(batch * seq, GRP), oc.reshape(batch * seq, GRP),
                    wa_b, wb_b, wc_b, wo_b, tm=tm_p, rows_per_mod=seq // tm_p)
        xp = _mlp(x1, mod_p, l, g_mlp, w1_b, w2_b, tm=tm_p, tf=1024, rows_per_mod=seq // tm_p)
        kp_l.append(kp)
        vp_l.append(vp)
        sp_l.append(s_fin)

        ps, ks, vs, vbs = _inproj(xs, mod_s, l, g_mix, w_in_b, cos_s, sin_s, g_qa_t, g_ka_t, g_vb_r, gsum,
                                  tm=rows_s, rows_per_mod=1, p_dtype=F32, hd=hd, dk=dk)
        q_s = ps[:, J_QA * GRP:(J_QA + 1) * GRP].reshape(nb, n_tok, 1, w_a)
        qbd = jnp.where(qbd_mask, jnp.broadcast_to(q_s, (nb, n_tok, n_heads_a, w_a)).reshape(nb, -1, w_a), 0.0)
        pad = ((0, 0), (0, page - n_tok), (0, 0))
        knew = jnp.pad(ks.reshape(nb, n_tok, w_a), pad).astype(BF16)
        vnew = jnp.pad(vs.reshape(nb, n_tok, w_a), pad).astype(BF16)
        oas = _paged_attn(page_table, qbd.astype(BF16), bias_col[l], knew, vnew, u_s, cache_k4, cache_v4,
                          layer=l, g_pages=g_pages, n_heads=n_heads_a, hd=hd, n_tok=n_tok)
        obs, ocs, s_new = _mix_sample(ps, state_ret, l, gpow_s, mg_s[l], bg_s[l], dmat_s, xi_sb, zeta_sb,
                                      n_batch=nb, n_tok=n_tok, n_heads=n_heads_c)
        x1s = _merge(xs, mod_s, l, ps, oas.reshape(rows_s, w_a), obs, ocs, wa_b, wb_b, wc_b, wo_b,
                     tm=rows_s, rows_per_mod=1)
        xs = _mlp(x1s, mod_s, l, g_mlp, w1_b, w2_b, tm=rows_s, tf=1024, rows_per_mod=1)
        ks_l.append(ks)
        vs_l.append(vs)
        ss_l.append(s_new)
        gv_l.append(vbs)

    def kv(lst, b_, l_):
        return jnp.stack(lst).reshape(depth, b_, l_, n_heads_a, hd)

    return (xp.reshape(batch, seq, d), xs.reshape(nb, n_tok, d),
            kv(kp_l, batch, seq), kv(vp_l, batch, seq), kv(ks_l, nb, n_tok), kv(vs_l, nb, n_tok),
            jnp.stack(sp_l), jnp.stack(ss_l), jnp.stack(gv_l).reshape(depth, nb, n_tok, w_b))
```

```python
import functools

import jax
import jax.numpy as jnp
import numpy as np
from jax import lax
from jax.experimental import pallas as pl
from jax.experimental.pallas import tpu as pltpu

F32 = jnp.float32
BF16 = jnp.bfloat16

EPS = 1e-6
LOG2E = float(np.log2(np.e))
ROPE_BASE = 10000.0
RET_CHUNK = 128
LANES = 128
V7X_VMEM_BYTES = 64 * 1024 * 1024
VMEM_LIMIT = 56 * 1024 * 1024

GRP = 512
J_GA, J_GB, J_GCM = 0, 2, 4
J_QA, J_KA, J_VA, J_UB, J_VB, J_QC, J_KC, J_VC, J_GC = 6, 7, 8, 9, 10, 11, 12, 13, 14
N_GRP = 15


def _cparams(sem, vmem=VMEM_LIMIT):
    return pltpu.CompilerParams(dimension_semantics=sem, vmem_limit_bytes=vmem)


def _resident(block_shape, index_map):
    return pl.BlockSpec(block_shape, index_map, pipeline_mode=pl.Buffered(1))


def _sigmoid(x):
    return 1.0 / (1.0 + jnp.exp(-x))


def _center_norm(o):
    mu = jnp.mean(o, axis=-1, keepdims=True)
    d = o - mu
    var = jnp.mean(d * d, axis=-1, keepdims=True)
    return d * lax.rsqrt(var + EPS)


def _gelu_tanh(x):
    return 0.5 * x * (1.0 + jnp.tanh(np.sqrt(2.0 / np.pi).astype(np.float32) * (x + 0.044715 * (x * x * x))))


def _nt_dot(a, b):
    return lax.dot_general(a, b, (((1,), (1,)), ((), ())), preferred_element_type=F32)


def _adaln_kernel(c_ref, w_ref, b_ref, o_ref):
    c = c_ref[...]
    s = (c * _sigmoid(c)).astype(BF16)
    o_ref[...] = jnp.dot(s, w_ref[...].astype(BF16), preferred_element_type=F32) + b_ref[...]


def _adaln(c_all, w_ada, b_ada, tn=1536):
    depth, d, n = w_ada.shape
    m = c_all.shape[0]
    return pl.pallas_call(
        _adaln_kernel,
        grid=(depth, n // tn),
        in_specs=[
            pl.BlockSpec((m, d), lambda l, j: (0, 0)),
            pl.BlockSpec((None, d, tn), lambda l, j: (l, 0, j)),
            pl.BlockSpec((None, 1, tn), lambda l, j: (l, 0, j)),
        ],
        out_specs=pl.BlockSpec((None, m, tn), lambda l, j: (l, 0, j)),
        out_shape=jax.ShapeDtypeStruct((depth, m, n), F32),
        compiler_params=_cparams(("parallel", "parallel")),
        name="adaln",
    )(c_all, w_ada, b_ada.reshape(depth, 1, n))


def _inproj_kernel(x_ref, shift_ref, scale_ref, g_ref, w_ref, cos_ref, sin_ref, gq_ref, gk_ref, gvb_ref,
                   gsum_ref, p_ref, k_ref, v_ref, *rest, hd_scale, kc_scale, with_vb):
    vb_ref = rest[0] if with_vb else None
    h_scr = rest[-1]
    x = x_ref[...]
    ms = jnp.mean(x * x, axis=-1, keepdims=True)
    y = x * lax.rsqrt(ms + EPS) * g_ref[...]
    h_scr[...] = (y * (1.0 + scale_ref[...]) + shift_ref[...]).astype(BF16)

    def head_norm(a, g):
        ms = jnp.dot((a * a).astype(BF16), gsum_ref[...], preferred_element_type=F32)
        return a * lax.rsqrt(ms + EPS) * g

    def rotary(a):
        outs = []
        for h in range(GRP // LANES):
            xh = a[:, h * LANES:(h + 1) * LANES]
            outs.append(xh * cos_ref[...] + pltpu.roll(xh, LANES // 2, axis=1) * sin_ref[...])
        return jnp.concatenate(outs, axis=1)

    def activation(j, acc):
        if j < J_QA:
            return _sigmoid(acc)
        if j == J_QA:
            return head_norm(acc, gq_ref[...]) * hd_scale
        if j == J_KA:
            kn = head_norm(acc, gk_ref[...])
            k_ref[...] = kn
            return kn
        if j == J_VA:
            v_ref[...] = acc
            return acc
        if j == J_UB:
            return _gelu_tanh(acc)
        if j == J_VB:
            vb = _center_norm(_gelu_tanh(acc)) * gvb_ref[...]
            if with_vb:
                vb_ref[...] = vb
            return vb
        if j == J_QC:
            return rotary(acc)
        if j == J_KC:
            return rotary(acc) * kc_scale
        if j == J_VC:
            return acc
        assert j == J_GC
        return acc * _sigmoid(acc)

    for j in range(N_GRP):
        cols = slice(j * GRP, (j + 1) * GRP)
        acc = jnp.dot(h_scr[...], w_ref[:, cols], preferred_element_type=F32)
        p_ref[:, cols] = activation(j, acc).astype(p_ref.dtype)


def _inproj(x2d, mod4, layer, g_norm, w_in_b, cos_t, sin_t, g_qa, g_ka, g_vb, gsum, *, tm, rows_per_mod,
            p_dtype, hd, dk, with_vb):
    m, d = x2d.shape
    n_in = w_in_b.shape[-1]
    assert n_in == N_GRP * GRP and m % tm == 0
    rm = mod4.shape[2]
    n_pos_blocks = cos_t.shape[0] // tm
    mod_idx = (lambda chunk: (lambda i: (layer, i // rows_per_mod, 0, chunk)))
    kern = functools.partial(_inproj_kernel, hd_scale=float(hd) ** -0.5 * LOG2E, kc_scale=float(dk) ** -0.5,
                             with_vb=with_vb)
    row_blk = lambda i: (i, 0)
    per_layer = lambda i: (layer, 0, 0)
    n_f32_out = 3 if with_vb else 2
    return pl.pallas_call(
        kern,
        grid=(m // tm,),
        in_specs=[
            pl.BlockSpec((tm, d), row_blk),
            pl.BlockSpec((None, None, rm, d), mod_idx(0)),
            pl.BlockSpec((None, None, rm, d), mod_idx(1)),
            pl.BlockSpec((None, 1, d), per_layer),
            _resident((None, d, n_in), per_layer),
            pl.BlockSpec((tm, LANES), lambda i: (i % n_pos_blocks, 0)),
            pl.BlockSpec((tm, LANES), lambda i: (i % n_pos_blocks, 0)),
            pl.BlockSpec((None, 1, GRP), per_layer),
            pl.BlockSpec((None, 1, GRP), per_layer),
            pl.BlockSpec((None, 1, GRP), per_layer),
            _resident((GRP, GRP), lambda i: (0, 0)),
        ],
        out_specs=[pl.BlockSpec((tm, n_in), row_blk)] + [pl.BlockSpec((tm, GRP), row_blk)] * n_f32_out,
        out_shape=[jax.ShapeDtypeStruct((m, n_in), p_dtype)] + [jax.ShapeDtypeStruct((m, GRP), F32)] * n_f32_out,
        scratch_shapes=[pltpu.VMEM((tm, d), BF16)],
        compiler_params=_cparams(("parallel",)),
        name="inproj",
    )(x2d, mod4, mod4, g_norm, w_in_b, cos_t, sin_t, g_qa, g_ka, g_vb, gsum)


def _neg_abs(x):
    return pltpu.bitcast(pltpu.bitcast(x, jnp.uint32) | jnp.uint32(0x80000000), F32)


def _sb_weights(z2, u_ext, c):
    n = z2.shape[1]
    sp = jnp.maximum(z2, 0.0) + jnp.log2(1.0 + jnp.exp2(_neg_abs(z2)))
    sums = jnp.dot(sp.astype(BF16), u_ext, preferred_element_type=F32)
    later = sums[:, :n] + jnp.concatenate([c] * (n // LANES), axis=1)
    a = jnp.exp2((z2 - sp - later).astype(BF16))
    return sums[:, n:], a


SB_NORMAL, SB_DIAG, SB_SKIP = 0, 1, 2


def _sb_attn_kernel(mb_ref, q_ref, k_ref, v_ref, u_ref, o_ref, z_scr, acc_scr, c_scr, *, tq, n_heads, hd,
                    pairs_per_body):
    i = pl.program_id(1)
    heads_per_vreg = LANES // hd
    n_pairs = n_heads // heads_per_vreg
    lane = lax.broadcasted_iota(jnp.int32, (1, LANES), 1)
    u = u_ref[...]
    in_head = [(lane >= sub * hd) & (lane < (sub + 1) * hd) for sub in range(heads_per_vreg)]

    for grp in range(n_pairs // pairs_per_body):
        pairs = range(grp * pairs_per_body, (grp + 1) * pairs_per_body)
        qh = {}
        for pair in pairs:
            q128 = q_ref[:, pair * LANES:(pair + 1) * LANES]
            for sub in range(heads_per_vreg):
                qh[pair, sub] = jnp.where(in_head[sub], q128, jnp.zeros_like(q128))
                h = pair * heads_per_vreg + sub
                acc_scr[h] = jnp.zeros((tq, LANES), F32)
                c_scr[h] = jnp.zeros((tq, LANES), F32)

        def scores(kb, slot, sel):
            start = pl.multiple_of(jnp.maximum(kb, 0) * tq, tq)
            for pair in pairs:
                k128 = k_ref[pl.ds(start, tq), pair * LANES:(pair + 1) * LANES]
                for sub in range(heads_per_vreg):
                    h = pair * heads_per_vreg + sub
                    z_scr[h, slot] = _nt_dot(qh[pair, sub], k128) + mb_ref[h, sel]

        def weigh(kb, slot):
            start = pl.multiple_of(jnp.maximum(kb, 0) * tq, tq)
            for pair in pairs:
                v128 = v_ref[pl.ds(start, tq), pair * LANES:(pair + 1) * LANES]
                for sub in range(heads_per_vreg):
                    h = pair * heads_per_vreg + sub
                    c = c_scr[h]
                    rs, a = _sb_weights(z_scr[h, slot], u, c)
                    acc_scr[h] += jnp.dot(a, v128, preferred_element_type=F32)
                    c_scr[h] = c + rs

        def skip_if_negative(kb):
            return jnp.where(kb < 0, SB_SKIP, SB_NORMAL)

        scores(i, 0, SB_DIAG)

        def body(m, carry):
            kb = i - 2 * m
            scores(kb - 1, 1, skip_if_negative(kb - 1))
            weigh(kb, 0)
            scores(kb - 2, 0, skip_if_negative(kb - 2))
            weigh(kb - 1, 1)
            return carry

        lax.fori_loop(0, (i + 2) // 2, body, 0)
        for pair in pairs:
            out128 = acc_scr[pair * heads_per_vreg]
            for sub in range(1, heads_per_vreg):
                out128 = jnp.where(in_head[sub], acc_scr[pair * heads_per_vreg + sub], out128)
            o_ref[:, pair * LANES:(pair + 1) * LANES] = out128.astype(o_ref.dtype)


def _sb_attn(p, mb, u, *, batch, seq, tq, n_heads, hd, pairs_per_body=1):
    nq = seq // tq
    width = n_heads * hd
    assert width == GRP
    kern = functools.partial(_sb_attn_kernel, tq=tq, n_heads=n_heads, hd=hd, pairs_per_body=pairs_per_body)
    return pl.pallas_call(
        kern,
        grid=(batch, nq),
        in_specs=[
            pl.BlockSpec((n_heads, 3, tq, tq), lambda b, i: (0, 0, 0, 0)),
            pl.BlockSpec((tq, GRP), lambda b, i: (b * nq + i, J_QA)),
            pl.BlockSpec((seq, GRP), lambda b, i: (b, J_KA)),
            pl.BlockSpec((seq, GRP), lambda b, i: (b, J_VA)),
            pl.BlockSpec((tq, tq + LANES), lambda b, i: (0, 0)),
        ],
        out_specs=pl.BlockSpec((tq, GRP), lambda b, i: (b * nq + i, 0)),
        out_shape=jax.ShapeDtypeStruct((batch * seq, GRP), BF16),
        scratch_shapes=[pltpu.VMEM((n_heads, 2, tq, tq), F32), pltpu.VMEM((n_heads, tq, LANES), F32),
                        pltpu.VMEM((n_heads, tq, LANES), F32)],
        compiler_params=_cparams(("parallel", "arbitrary")),
        name="sb_attn",
    )(mb, p, p, p, u)


PAGED_CHUNK = 256


def _paged_attn_kernel(pt_ref, q_ref, mb_ref, mbnew_ref, knew_ref, vnew_ref, u_ref, unew_ref, ck_hbm, cv_hbm, o_ref,
                       kbuf, vbuf, sem, *, layer, n_groups, g_pages, page, n_heads, hd):
    b = pl.program_id(0)
    nb = pl.num_programs(0)
    n_pages = n_groups * g_pages
    flat = page * n_heads
    n_chunks = flat // PAGED_CHUNK

    def copies(bb, gi, slot):
        out = []
        for pi in range(g_pages):
            pg = pt_ref[bb, n_pages - (gi + 1) * g_pages + pi]
            out.append(pltpu.make_async_copy(ck_hbm.at[layer, pg], kbuf.at[slot, pi], sem.at[0, slot]))
            out.append(pltpu.make_async_copy(cv_hbm.at[layer, pg], vbuf.at[slot, pi], sem.at[1, slot]))
        return out

    @pl.when(b == 0)
    def _():
        for cp in copies(0, 0, 0):
            cp.start()

    q = q_ref[...]
    rows = q.shape[0]

    rs, a = _sb_weights(_nt_dot(q, knew_ref[...]) + mbnew_ref[...], unew_ref[...], jnp.zeros((rows, LANES), F32))
    c = rs
    acc = jnp.dot(a, vnew_ref[...], preferred_element_type=F32)

    def one_page(kf, vf, c, acc):
        z2 = _nt_dot(q, kf) + mb_ref[...]
        sp = jnp.maximum(z2, 0.0) + jnp.log2(1.0 + jnp.exp2(_neg_abs(z2)))
        sp_st = jnp.concatenate([sp[:, j * PAGED_CHUNK:(j + 1) * PAGED_CHUNK] for j in range(n_chunks)], axis=0)
        sums = jnp.dot(sp_st.astype(BF16), u_ref[...], preferred_element_type=F32)
        later = []
        for j in reversed(range(n_chunks)):
            blk = sums[j * rows:(j + 1) * rows]
            later.append(blk[:, :PAGED_CHUNK] + jnp.concatenate([c] * (PAGED_CHUNK // LANES), axis=1))
            c = c + blk[:, PAGED_CHUNK:]
        later = jnp.concatenate(later[::-1], axis=1)
        a = jnp.exp2((z2 - sp - later).astype(BF16))
        return c, acc + jnp.dot(a, vf, preferred_element_type=F32)

    def body(gi, carry):
        n = b * n_groups + gi
        slot = n % 2
        last = gi == n_groups - 1

        @pl.when(jnp.logical_not(last))
        def _():
            for cp in copies(b, gi + 1, 1 - slot):
                cp.start()

        @pl.when(last & (b + 1 < nb))
        def _():
            for cp in copies(b + 1, 0, 1 - slot):
                cp.start()

        for cp in copies(b, gi, slot):
            cp.wait()
        c, acc = carry
        for pi in reversed(range(g_pages)):
            kf = kbuf[slot, pi].reshape(flat, hd).astype(BF16)
            vf = vbuf[slot, pi].reshape(flat, hd).astype(BF16)
            c, acc = one_page(kf, vf, c, acc)
        return c, acc

    c, acc = lax.fori_loop(0, n_groups, body, (c, acc))
    o_ref[...] = acc


def _paged_attn(page_table, q_rows, mb, mb_new, knew, vnew, u_ext, u_new, cache_k, cache_v, *, layer, g_pages):
    nb, n_pages = page_table.shape
    _, _, page, n_heads, hd = cache_k.shape
    rows = q_rows.shape[1]
    flat = page * n_heads
    n_new = knew.shape[1]
    assert n_pages % g_pages == 0 and flat % PAGED_CHUNK == 0
    kern = functools.partial(_paged_attn_kernel, layer=layer, n_groups=n_pages // g_pages, g_pages=g_pages,
                             page=page, n_heads=n_heads, hd=hd)
    const2 = lambda b, pt: (0, 0)
    per_b = lambda b, pt: (b, 0, 0)
    grid_spec = pltpu.PrefetchScalarGridSpec(
        num_scalar_prefetch=1,
        grid=(nb,),
        in_specs=[
            pl.BlockSpec((None, rows, hd), per_b),
            pl.BlockSpec((rows, flat), const2),
            pl.BlockSpec((rows, n_new), const2),
            pl.BlockSpec((None, n_new, hd), per_b),
            pl.BlockSpec((None, n_new, hd), per_b),
            pl.BlockSpec((PAGED_CHUNK, PAGED_CHUNK + LANES), const2),
            pl.BlockSpec((n_new, n_new + LANES), const2),
            pl.BlockSpec(memory_space=pl.ANY),
            pl.BlockSpec(memory_space=pl.ANY),
        ],
        out_specs=pl.BlockSpec((None, rows, hd), per_b),
        scratch_shapes=[
            pltpu.VMEM((2, g_pages, page, n_heads, hd), F32),
            pltpu.VMEM((2, g_pages, page, n_heads, hd), F32),
            pltpu.SemaphoreType.DMA((2, 2)),
        ],
    )
    return pl.pallas_call(
        kern,
        grid_spec=grid_spec,
        out_shape=jax.ShapeDtypeStruct((nb, rows, hd), F32),
        compiler_params=_cparams(("arbitrary",)),
        name="paged_attn",
    )(page_table, q_rows, mb, mb_new, knew, vnew, u_ext, u_new, cache_k, cache_v)


def _mix_prompt_kernel(gpow_ref, ub_ref, vb_ref, qc_ref, kc_ref, vc_ref, gc_ref, wsp_ref, bsp_ref, decay_ref,
                       xi_ref, zeta_ref, ob_ref, oc_ref, sfin_ref, state, *, batch, n_grp, n_heads, t):
    ci = pl.program_id(0)

    @pl.when(ci == 0)
    def _():
        state[...] = jnp.zeros_like(state)

    r = lax.broadcasted_iota(jnp.int32, (t, t), 0)
    s = lax.broadcasted_iota(jnp.int32, (t, t), 1)
    tril = s <= r
    for b in range(batch):
        for g in range(n_grp):
            lanes = slice(g * LANES, (g + 1) * LANES)
            w = jnp.where(tril, wsp_ref[g], 0.0).astype(BF16)
            mixed = jnp.dot(w, vb_ref[b, :, lanes], preferred_element_type=F32) + bsp_ref[g]
            ob_ref[b, :, lanes] = (ub_ref[b, :, lanes].astype(F32) * mixed).astype(ob_ref.dtype)
        for h in range(n_heads):
            lanes = slice(h * LANES, (h + 1) * LANES)
            q = qc_ref[b, :, lanes]
            k = kc_ref[b, :, lanes]
            v = vc_ref[b, :, lanes]
            st = state[b, h]
            scores = _nt_dot(q, k) * decay_ref[h]
            inner = jnp.dot(scores.astype(BF16), v, preferred_element_type=F32)
            cross = jnp.dot(q, st.astype(BF16), preferred_element_type=F32) * xi_ref[h]
            kz = (k.astype(F32) * zeta_ref[h]).astype(BF16)
            upd = lax.dot_general(kz, v, (((0,), (0,)), ((), ())), preferred_element_type=F32)
            state[b, h] = gpow_ref[h] * st + upd
            on = _center_norm(inner + cross)
            oc_ref[b, :, lanes] = (on * gc_ref[b, :, lanes].astype(F32)).astype(oc_ref.dtype)

    @pl.when(ci == pl.num_programs(0) - 1)
    def _():
        sfin_ref[...] = state[...]


def _mix_prompt(p3, gpow, wsp_l, bsp_b, decay, xi_b, zeta_b, *, n_grp, n_heads):
    batch, seq, _ = p3.shape
    t = RET_CHUNK
    kern = functools.partial(_mix_prompt_kernel, batch=batch, n_grp=n_grp, n_heads=n_heads, t=t)
    col = lambda jj: (lambda c: (0, c, jj))
    const3 = lambda c: (0, 0, 0)
    return pl.pallas_call(
        kern,
        grid=(seq // t,),
        in_specs=[
            pl.BlockSpec(memory_space=pltpu.SMEM),
            pl.BlockSpec((batch, t, GRP), col(J_UB)),
            pl.BlockSpec((batch, t, GRP), col(J_VB)),
            pl.BlockSpec((batch, t, GRP), col(J_QC)),
            pl.BlockSpec((batch, t, GRP), col(J_KC)),
            pl.BlockSpec((batch, t, GRP), col(J_VC)),
            pl.BlockSpec((batch, t, GRP), col(J_GC)),
            pl.BlockSpec((n_grp, t, t), const3),
            pl.BlockSpec((n_grp, t, LANES), const3),
            pl.BlockSpec((n_heads, t, t), const3),
            pl.BlockSpec((n_heads, t, LANES), const3),
            pl.BlockSpec((n_heads, t, LANES), const3),
        ],
        out_specs=[
            pl.BlockSpec((batch, t, GRP), lambda c: (0, c, 0)),
            pl.BlockSpec((batch, t, GRP), lambda c: (0, c, 0)),
            pl.BlockSpec((batch, n_heads, LANES, LANES), lambda c: (0, 0, 0, 0)),
        ],
        out_shape=[
            jax.ShapeDtypeStruct((batch, seq, GRP), BF16),
            jax.ShapeDtypeStruct((batch, seq, GRP), BF16),
            jax.ShapeDtypeStruct((batch, n_heads, LANES, LANES), F32),
        ],
        scratch_shapes=[pltpu.VMEM((batch, n_heads, LANES, LANES), F32)],
        compiler_params=_cparams(("arbitrary",)),
        name="mix_prompt",
    )(gpow, p3, p3, p3, p3, p3, p3, wsp_l, bsp_b, decay, xi_b, zeta_b)


def _mix_sample_kernel(gpow_ref, ub_ref, vb_ref, qc_ref, kc_ref, vc_ref, gc_ref, mg_ref, bg_ref, dmat_ref,
                       xi_ref, zeta_ref, st_ref, ob_ref, oc_ref, snew_ref, *, n_batch, n_tok):
    h = pl.program_id(0)
    rows = n_batch * n_tok
    mixed = jnp.dot(mg_ref[...].astype(BF16), vb_ref[...].astype(BF16), preferred_element_type=F32) + bg_ref[...]
    ob_ref[...] = ub_ref[...] * mixed

    q = qc_ref[...].astype(BF16)
    k = kc_ref[...]
    v = vc_ref[...].astype(BF16)
    scores = _nt_dot(q, k.astype(BF16)) * dmat_ref[...]
    inner = jnp.dot(scores.astype(BF16), v, preferred_element_type=F32)
    kzt = (k * zeta_ref[...]).T
    gp = gpow_ref[h]
    row_b = lax.broadcasted_iota(jnp.int32, (rows, LANES), 0) // n_tok
    col_b = lax.broadcasted_iota(jnp.int32, (LANES, rows), 1) // n_tok

    def body(b, cross):
        st = st_ref[b]
        cr = jnp.dot(q, st.astype(BF16), preferred_element_type=F32)
        cross = jnp.where(row_b == b, cr, cross)
        upd = jnp.dot(jnp.where(col_b == b, kzt, 0.0).astype(BF16), v, preferred_element_type=F32)
        snew_ref[b] = gp * st + upd
        return cross

    cross = lax.fori_loop(0, n_batch, body, jnp.zeros((rows, LANES), F32))
    on = _center_norm(inner + cross * xi_ref[...])
    oc_ref[...] = on * gc_ref[...]


def _mix_sample(ps, state_ret, layer, gpow4, mg, bg, dmat, xi_s, zeta_s, *, n_batch, n_tok, n_heads):
    rows = n_batch * n_tok
    kern = functools.partial(_mix_sample_kernel, n_batch=n_batch, n_tok=n_tok)
    col = lambda jj: (lambda h: (0, jj * (GRP // LANES) + h))
    per_h = lambda h: (h, 0, 0)
    return pl.pallas_call(
        kern,
        grid=(n_heads,),
        in_specs=[
            pl.BlockSpec(memory_space=pltpu.SMEM),
            pl.BlockSpec((rows, LANES), col(J_UB)),
            pl.BlockSpec((rows, LANES), col(J_VB)),
            pl.BlockSpec((rows, LANES), col(J_QC)),
            pl.BlockSpec((rows, LANES), col(J_KC)),
            pl.BlockSpec((rows, LANES), col(J_VC)),
            pl.BlockSpec((rows, LANES), col(J_GC)),
            pl.BlockSpec((None, rows, rows), per_h),
            pl.BlockSpec((None, rows, LANES), per_h),
            pl.BlockSpec((None, rows, rows), per_h),
            pl.BlockSpec((None, rows, LANES), per_h),
            pl.BlockSpec((None, rows, LANES), per_h),
            pl.BlockSpec((None, n_batch, None, LANES, LANES), lambda h: (layer, 0, h, 0, 0)),
        ],
        out_specs=[
            pl.BlockSpec((rows, LANES), lambda h: (0, h)),
            pl.BlockSpec((rows, LANES), lambda h: (0, h)),
            pl.BlockSpec((n_batch, None, LANES, LANES), lambda h: (0, h, 0, 0)),
        ],
        out_shape=[
            jax.ShapeDtypeStruct((rows, GRP), F32),
            jax.ShapeDtypeStruct((rows, GRP), F32),
            jax.ShapeDtypeStruct((n_batch, n_heads, LANES, LANES), F32),
        ],
        compiler_params=_cparams(("parallel",)),
        name="mix_sample",
    )(gpow4, ps, ps, ps, ps, ps, ps, mg, bg, dmat, xi_s, zeta_s, state_ret)


def _merge_mlp_kernel(x_ref, gate1_ref, shift_ref, scale_ref, gate2_ref, g_ref, oa_ref, ob_ref, oc_ref, ga_ref,
                      gb_ref, gcm_ref, wa_ref, wb_ref, wc_ref, wo_ref, w1_ref, w2_ref, o_ref, h_scr, *, tf):
    def br(o_r, w_r, g_r):
        return g_r[...].astype(F32) * jnp.dot(o_r[...].astype(BF16), w_r[...], preferred_element_type=F32)

    merged = br(oa_ref, wa_ref, ga_ref) + br(ob_ref, wb_ref, gb_ref) + br(oc_ref, wc_ref, gcm_ref)
    x1 = x_ref[...] + gate1_ref[...] * jnp.dot(merged.astype(BF16), wo_ref[...], preferred_element_type=F32)
    o_ref[...] = x1
    ms = jnp.mean(x1 * x1, axis=-1, keepdims=True)
    y = x1 * lax.rsqrt(ms + EPS) * g_ref[...]
    h_scr[...] = (y * (1.0 + scale_ref[...]) + shift_ref[...]).astype(BF16)
    acc = None
    for j in range(w1_ref.shape[1] // tf):
        a = jnp.maximum(jnp.dot(h_scr[...], w1_ref[:, j * tf:(j + 1) * tf], preferred_element_type=F32), 0.0)
        t = jnp.dot((a * a).astype(BF16), w2_ref[j * tf:(j + 1) * tf, :], preferred_element_type=F32)
        acc = t if acc is None else acc + t
    o_ref[...] = o_ref[...] + gate2_ref[...] * acc


def _merge_mlp(x2d, mod4, layer, g_norm, p, oa, ob, oc, wa, wb, wc, wo, w1, w2, *, tm, tf, rows_per_mod):
    m, d = x2d.shape
    rm = mod4.shape[2]
    row = lambda i: (i, 0)
    mod_idx = (lambda chunk: (lambda i: (layer, i // rows_per_mod, 0, chunk)))
    wspec = lambda w: _resident((None,) + w.shape[1:], lambda i: (layer, 0, 0))
    return pl.pallas_call(
        functools.partial(_merge_mlp_kernel, tf=tf),
        grid=(m // tm,),
        in_specs=[
            pl.BlockSpec((tm, d), row),
            pl.BlockSpec((None, None, rm, d), mod_idx(2)),
            pl.BlockSpec((None, None, rm, d), mod_idx(3)),
            pl.BlockSpec((None, None, rm, d), mod_idx(4)),
            pl.BlockSpec((None, None, rm, d), mod_idx(5)),
            pl.BlockSpec((None, 1, d), lambda i: (layer, 0, 0)),
            pl.BlockSpec((tm, GRP), row),
            pl.BlockSpec((tm, GRP), row),
            pl.BlockSpec((tm, GRP), row),
            pl.BlockSpec((tm, d), lambda i: (i, J_GA // 2)),
            pl.BlockSpec((tm, d), lambda i: (i, J_GB // 2)),
            pl.BlockSpec((tm, d), lambda i: (i, J_GCM // 2)),
            wspec(wa), wspec(wb), wspec(wc), wspec(wo), wspec(w1), wspec(w2),
        ],
        out_specs=pl.BlockSpec((tm, d), row),
        out_shape=jax.ShapeDtypeStruct((m, d), F32),
        scratch_shapes=[pltpu.VMEM((tm, d), BF16)],
        compiler_params=_cparams(("parallel",)),
        name="merge_mlp",
    )(x2d, mod4, mod4, mod4, mod4, g_norm, oa, ob, oc, p, p, p, wa, wb, wc, wo, w1, w2)


def _rope_tables(pos, dk):
    half = dk // 2
    freqs = ROPE_BASE ** (-jnp.arange(half, dtype=F32) / half)
    ang = pos.astype(F32)[:, None] * freqs[None, :]
    cos, sin = jnp.cos(ang), jnp.sin(ang)
    return jnp.concatenate([cos, cos], axis=1), jnp.concatenate([-sin, sin], axis=1)


def kernel(x_prompt, x_sample, cache_k, cache_v, state_ret, page_table, c_prompt, c_sample, w_ada, b_ada, g_norm_mix, g_norm_mlp, w_in, g_qa, g_ka, sb_bias, g_vb, w_spatial, b_spatial, w_proj_a, w_proj_b, w_proj_c, w_out, w_mlp1, w_mlp2):
    batch, seq, d = x_prompt.shape
    nb, n_tok, _ = x_sample.shape
    depth, n_pool, page, n_heads_a, hd = cache_k.shape
    n_pages = page_table.shape[1]
    past = n_pages * page
    n_grp_b, chunk_b = w_spatial.shape[1], w_spatial.shape[2]
    w_b = g_vb.shape[1]
    n_heads_c, dk, dv = state_ret.shape[2:]
    w_a = n_heads_a * hd
    w_c = n_heads_c * dv
    rows_s = nb * n_tok
    assert w_a == GRP and w_b == GRP and w_c == GRP and n_heads_c * dk == GRP and d == 2 * GRP
    assert dk == LANES and dv == LANES and w_b // n_grp_b == LANES and chunk_b == RET_CHUNK and n_grp_b == n_heads_c
    assert seq % RET_CHUNK == 0 and n_tok <= chunk_b

    sizes = (w_a, w_a, w_a, w_b, w_b, w_c, w_c, w_c, w_c, d, d, d)
    offs = np.concatenate([[0], np.cumsum(sizes)])
    order = (9, 10, 11, 0, 1, 2, 3, 4, 5, 6, 7, 8)
    w_in_b = jnp.concatenate([w_in[:, :, offs[o]:offs[o + 1]] for o in order], axis=-1).astype(BF16)
    wa_b, wb_b, wc_b, wo_b = (w.astype(BF16) for w in (w_proj_a, w_proj_b, w_proj_c, w_out))
    w1_b, w2_b = w_mlp1.astype(BF16), w_mlp2.astype(BF16)
    g_mix = g_norm_mix.reshape(depth, 1, d)
    g_mlp = g_norm_mlp.reshape(depth, 1, d)
    g_qa_t = jnp.tile(g_qa, (1, n_heads_a)).reshape(depth, 1, GRP)
    g_ka_t = jnp.tile(g_ka, (1, n_heads_a)).reshape(depth, 1, GRP)
    g_vb_r = g_vb.reshape(depth, 1, GRP)
    gi = np.arange(GRP) // hd
    gsum = jnp.asarray((gi[:, None] == gi[None, :]).astype(np.float32) / hd, dtype=BF16)

    log_gamma = jnp.log1p(-jnp.exp2(-5.0 - jnp.arange(n_heads_c, dtype=F32)))

    def ret_tables(length):
        i = jnp.arange(length, dtype=F32)
        diff = i[:, None] - i[None, :]
        decay = jnp.where(diff >= 0, jnp.exp(log_gamma[:, None, None] * jnp.maximum(diff, 0.0)), 0.0)
        xi = jnp.exp(log_gamma[:, None] * (i[None, :] + 1.0))
        zeta = jnp.exp(log_gamma[:, None] * (length - 1.0 - i[None, :]))
        return decay, xi, zeta, jnp.exp(log_gamma * length)

    decay_p, xi_p, zeta_p, gpow_p = ret_tables(RET_CHUNK)
    xi_pb = jnp.broadcast_to(xi_p[:, :, None], (n_heads_c, RET_CHUNK, LANES))
    zeta_pb = jnp.broadcast_to(zeta_p[:, :, None], (n_heads_c, RET_CHUNK, LANES))
    decay_s, xi_s, zeta_s, gpow_s = ret_tables(n_tok)
    eye_b = jnp.eye(nb, dtype=F32)
    dmat_s = jnp.stack([jnp.kron(eye_b, decay_s[h]) for h in range(n_heads_c)])
    xi_sb = jnp.broadcast_to(jnp.tile(xi_s, (1, nb))[:, :, None], (n_heads_c, rows_s, LANES))
    zeta_sb = jnp.broadcast_to(jnp.tile(zeta_s, (1, nb))[:, :, None], (n_heads_c, rows_s, LANES))

    bsp_pb = jnp.broadcast_to(b_spatial[:, :, :, None], (depth, n_grp_b, chunk_b, LANES))
    w_sp_s = jnp.tril(w_spatial[:, :, :n_tok, :n_tok])
    mg_s = jnp.einsum("ab,lgts->lgatbs", eye_b, w_sp_s).reshape(depth, n_grp_b, rows_s, rows_s)
    bg_s = jnp.broadcast_to(jnp.tile(b_spatial[:, :, :n_tok], (1, 1, nb))[..., None],
                            (depth, n_grp_b, rows_s, LANES))

    cos_p, sin_p = _rope_tables(jnp.arange(seq), dk)
    cos_s, sin_s = _rope_tables(past + (jnp.arange(rows_s) % n_tok), dk)

    def later_sum_matrix(n):
        tri = np.arange(n)
        strict = (tri[:, None] > tri[None, :]).astype(np.float32)
        return jnp.asarray(np.concatenate([strict, np.ones((n, LANES), np.float32)], axis=1), dtype=BF16)

    tq = 256
    g_pages = 4
    new_keys = 16
    assert n_tok <= new_keys and tq == PAGED_CHUNK
    u_p = later_sum_matrix(tq)
    u_new = later_sum_matrix(new_keys * n_heads_a)

    m_all = batch + nb
    m_pad = -(-m_all // 8) * 8
    c_all = jnp.concatenate([c_prompt, c_sample, jnp.zeros((m_pad - m_all, d), F32)], axis=0)
    mod = _adaln(c_all, w_ada, b_ada)
    mod_p = mod[:, :batch].reshape(depth, batch, 1, 6 * d)
    mod_s = jnp.repeat(mod[:, batch:m_all], n_tok, axis=1).reshape(depth, 1, rows_s, 6 * d)

    sb_bias2 = sb_bias * LOG2E
    bias_full = jnp.broadcast_to(sb_bias2[:, :, None, None], (depth, n_heads_a, tq, tq))
    neg_inf = jnp.full_like(bias_full, -jnp.inf)
    strictly_causal = jnp.asarray(np.arange(tq)[None, :] < np.arange(tq)[:, None])
    mb_p = jnp.stack([bias_full, jnp.where(strictly_causal, bias_full, neg_inf), neg_inf], axis=2)
    q_head = np.arange(n_tok * n_heads_a) % n_heads_a
    q_tok = np.arange(n_tok * n_heads_a) // n_heads_a
    row_bias = sb_bias2[:, q_head]
    own_page = jnp.asarray(q_head[:, None] == (np.arange(page * n_heads_a) % n_heads_a)[None, :])
    mb_s = jnp.where(own_page[None], row_bias[:, :, None], -jnp.inf)
    f_new = np.arange(new_keys * n_heads_a)
    own_new = jnp.asarray((q_head[:, None] == (f_new % n_heads_a)[None, :])
                          & ((f_new // n_heads_a)[None, :] < q_tok[:, None]))
    mb_new = jnp.where(own_new[None], row_bias[:, :, None], -jnp.inf)

    tm_p = 512
    xp = x_prompt.reshape(batch * seq, d)
    xs = x_sample.reshape(rows_s, d)
    kp_l, vp_l, ks_l, vs_l, sp_l, ss_l, gv_l = [], [], [], [], [], [], []
    for l in range(depth):
        p, kp, vp = _inproj(xp, mod_p, l, g_mix, w_in_b, cos_p, sin_p, g_qa_t, g_ka_t, g_vb_r, gsum,
                            tm=tm_p, rows_per_mod=seq // tm_p, p_dtype=BF16, hd=hd, dk=dk, with_vb=False)
        oa = _sb_attn(p, mb_p[l], u_p, batch=batch, seq=seq, tq=tq, n_heads=n_heads_a, hd=hd, pairs_per_body=2)
        ob, oc, s_fin = _mix_prompt(p.reshape(batch, seq, -1), gpow_p, w_spatial[l], bsp_pb[l], decay_p, xi_pb,
                                    zeta_pb, n_grp=n_grp_b, n_heads=n_heads_c)
        xp = _merge_mlp(xp, mod_p, l, g_mlp, p, oa, ob.reshape(batch * seq, GRP), oc.reshape(batch * seq, GRP),
                        wa_b, wb_b, wc_b, wo_b, w1_b, w2_b, tm=tm_p, tf=1024, rows_per_mod=seq // tm_p)
        kp_l.append(kp)
        vp_l.append(vp)
        sp_l.append(s_fin)

        ps, ks, vs, vbs = _inproj(xs, mod_s, l, g_mix, w_in_b, cos_s, sin_s, g_qa_t, g_ka_t, g_vb_r, gsum,
                                  tm=rows_s, rows_per_mod=1, p_dtype=F32, hd=hd, dk=dk, with_vb=True)
        q_rows = ps[:, J_QA * GRP:(J_QA + 1) * GRP].reshape(nb, n_tok * n_heads_a, hd).astype(BF16)
        pad = ((0, 0), (0, new_keys - n_tok), (0, 0))
        knew = jnp.pad(ks.reshape(nb, n_tok, w_a), pad).reshape(nb, new_keys * n_heads_a, hd).astype(BF16)
        vnew = jnp.pad(vs.reshape(nb, n_tok, w_a), pad).reshape(nb, new_keys * n_heads_a, hd).astype(BF16)
        oas = _paged_attn(page_table, q_rows, mb_s[l], mb_new[l], knew, vnew, u_p, u_new, cache_k, cache_v,
                          layer=l, g_pages=g_pages)
        obs, ocs, s_new = _mix_sample(ps, state_ret, l, gpow_s, mg_s[l], bg_s[l], dmat_s, xi_sb, zeta_sb,
                                      n_batch=nb, n_tok=n_tok, n_heads=n_heads_c)
        xs = _merge_mlp(xs, mod_s, l, g_mlp, ps, oas.reshape(rows_s, w_a), obs, ocs, wa_b, wb_b, wc_b, wo_b,
                        w1_b, w2_b, tm=rows_s, tf=1024, rows_per_mod=1)
        ks_l.append(ks)
        vs_l.append(vs)
        ss_l.append(s_new)
        gv_l.append(vbs)

    def kv(lst, b_, l_):
        return jnp.stack(lst).reshape(depth, b_, l_, n_heads_a, hd)

    return (xp.reshape(batch, seq, d), xs.reshape(nb, n_tok, d),
            kv(kp_l, batch, seq), kv(vp_l, batch, seq), kv(ks_l, nb, n_tok), kv(vs_l, nb, n_tok),
            jnp.stack(sp_l), jnp.stack(ss_l), jnp.stack(gv_l).reshape(depth, nb, n_tok, w_b))
```

```python
import functools

import jax
import jax.numpy as jnp
import numpy as np
from jax import lax
from jax.experimental import pallas as pl
from jax.experimental.pallas import tpu as pltpu

F32 = jnp.float32
BF16 = jnp.bfloat16

EPS = 1e-6
LOG2E = float(np.log2(np.e))
ROPE_BASE = 10000.0
RET_CHUNK = 128
LANES = 128
V7X_VMEM_BYTES = 64 * 1024 * 1024
VMEM_LIMIT = 56 * 1024 * 1024

GRP = 512
J_GA, J_GB, J_GCM = 0, 2, 4
J_QA, J_KA, J_VA, J_UB, J_VB, J_QC, J_KC, J_VC, J_GC = 6, 7, 8, 9, 10, 11, 12, 13, 14
N_GRP = 15


def _cparams(sem, vmem=VMEM_LIMIT):
    return pltpu.CompilerParams(dimension_semantics=sem, vmem_limit_bytes=vmem)


def _resident(block_shape, index_map):
    return pl.BlockSpec(block_shape, index_map, pipeline_mode=pl.Buffered(1))


def _sigmoid(x):
    return 1.0 / (1.0 + jnp.exp(-x))


def _center_norm(o):
    mu = jnp.mean(o, axis=-1, keepdims=True)
    d = o - mu
    var = jnp.mean(d * d, axis=-1, keepdims=True)
    return d * lax.rsqrt(var + EPS)


def _gelu_tanh(x):
    return 0.5 * x * (1.0 + jnp.tanh(np.sqrt(2.0 / np.pi).astype(np.float32) * (x + 0.044715 * (x * x * x))))


def _nt_dot(a, b):
    return lax.dot_general(a, b, (((1,), (1,)), ((), ())), preferred_element_type=F32)


def _adaln_kernel(c_ref, w_ref, b_ref, o_ref):
    c = c_ref[...]
    s = (c * _sigmoid(c)).astype(BF16)
    o_ref[...] = jnp.dot(s, w_ref[...].astype(BF16), preferred_element_type=F32) + b_ref[...]


def _adaln(c_all, w_ada, b_ada, tn=1536):
    depth, d, n = w_ada.shape
    m = c_all.shape[0]
    return pl.pallas_call(
        _adaln_kernel,
        grid=(depth, n // tn),
        in_specs=[
            pl.BlockSpec((m, d), lambda l, j: (0, 0)),
            pl.BlockSpec((None, d, tn), lambda l, j: (l, 0, j)),
            pl.BlockSpec((None, 1, tn), lambda l, j: (l, 0, j)),
        ],
        out_specs=pl.BlockSpec((None, m, tn), lambda l, j: (l, 0, j)),
        out_shape=jax.ShapeDtypeStruct((depth, m, n), F32),
        compiler_params=_cparams(("parallel", "parallel")),
        name="adaln",
    )(c_all, w_ada, b_ada.reshape(depth, 1, n))


def _inproj_kernel(x_ref, shift_ref, scale_ref, g_ref, w_ref, cos_ref, sin_ref, gq_ref, gk_ref, gvb_ref,
                   gsum_ref, p_ref, k_ref, v_ref, *rest, hd_scale, kc_scale, with_vb):
    vb_ref = rest[0] if with_vb else None
    h_scr = rest[-1]
    x = x_ref[...]
    ms = jnp.mean(x * x, axis=-1, keepdims=True)
    y = x * lax.rsqrt(ms + EPS) * g_ref[...]
    h_scr[...] = (y * (1.0 + scale_ref[...]) + shift_ref[...]).astype(BF16)

    def head_norm(a, g):
        ms = jnp.dot((a * a).astype(BF16), gsum_ref[...], preferred_element_type=F32)
        return a * lax.rsqrt(ms + EPS) * g

    def rotary(a):
        outs = []
        for h in range(GRP // LANES):
            xh = a[:, h * LANES:(h + 1) * LANES]
            outs.append(xh * cos_ref[...] + pltpu.roll(xh, LANES // 2, axis=1) * sin_ref[...])
        return jnp.concatenate(outs, axis=1)

    def activation(j, acc):
        if j < J_QA:
            return _sigmoid(acc)
        if j == J_QA:
            return head_norm(acc, gq_ref[...]) * hd_scale
        if j == J_KA:
            kn = head_norm(acc, gk_ref[...])
            k_ref[...] = kn
            return kn
        if j == J_VA:
            v_ref[...] = acc
            return acc
        if j == J_UB:
            return _gelu_tanh(acc)
        if j == J_VB:
            vb = _center_norm(_gelu_tanh(acc)) * gvb_ref[...]
            if with_vb:
                vb_ref[...] = vb
            return vb
        if j == J_QC:
            return rotary(acc)
        if j == J_KC:
            return rotary(acc) * kc_scale
        if j == J_VC:
            return acc
        assert j == J_GC
        return acc * _sigmoid(acc)

    for j in range(N_GRP):
        cols = slice(j * GRP, (j + 1) * GRP)
        acc = jnp.dot(h_scr[...], w_ref[:, cols], preferred_element_type=F32)
        p_ref[:, cols] = activation(j, acc).astype(p_ref.dtype)


def _inproj(x2d, mod4, layer, g_norm, w_in_b, cos_t, sin_t, g_qa, g_ka, g_vb, gsum, *, tm, rows_per_mod,
            p_dtype, hd, dk, with_vb):
    m, d = x2d.shape
    n_in = w_in_b.shape[-1]
    assert n_in == N_GRP * GRP and m % tm == 0
    rm = mod4.shape[2]
    n_pos_blocks = cos_t.shape[0] // tm
    mod_idx = (lambda chunk: (lambda i: (layer, i // rows_per_mod, 0, chunk)))
    kern = functools.partial(_inproj_kernel, hd_scale=float(hd) ** -0.5 * LOG2E, kc_scale=float(dk) ** -0.5,
                             with_vb=with_vb)
    row_blk = lambda i: (i, 0)
    per_layer = lambda i: (layer, 0, 0)
    n_f32_out = 3 if with_vb else 2
    return pl.pallas_call(
        kern,
        grid=(m // tm,),
        in_specs=[
            pl.BlockSpec((tm, d), row_blk),
            pl.BlockSpec((None, None, rm, d), mod_idx(0)),
            pl.BlockSpec((None, None, rm, d), mod_idx(1)),
            pl.BlockSpec((None, 1, d), per_layer),
            _resident((None, d, n_in), per_layer),
            pl.BlockSpec((tm, LANES), lambda i: (i % n_pos_blocks, 0)),
            pl.BlockSpec((tm, LANES), lambda i: (i % n_pos_blocks, 0)),
            pl.BlockSpec((None, 1, GRP), per_layer),
            pl.BlockSpec((None, 1, GRP), per_layer),
            pl.BlockSpec((None, 1, GRP), per_layer),
            _resident((GRP, GRP), lambda i: (0, 0)),
        ],
        out_specs=[pl.BlockSpec((tm, n_in), row_blk)] + [pl.BlockSpec((tm, GRP), row_blk)] * n_f32_out,
        out_shape=[jax.ShapeDtypeStruct((m, n_in), p_dtype)] + [jax.ShapeDtypeStruct((m, GRP), F32)] * n_f32_out,
        scratch_shapes=[pltpu.VMEM((tm, d), BF16)],
        compiler_params=_cparams(("parallel",)),
        name="inproj",
    )(x2d, mod4, mod4, g_norm, w_in_b, cos_t, sin_t, g_qa, g_ka, g_vb, gsum)


def _neg_abs(x):
    return pltpu.bitcast(pltpu.bitcast(x, jnp.uint32) | jnp.uint32(0x80000000), F32)


def _sb_weights(z2, u_ext, c):
    n = z2.shape[1]
    sp = jnp.maximum(z2, 0.0) + jnp.log2(1.0 + jnp.exp2(_neg_abs(z2)))
    sums = jnp.dot(sp.astype(BF16), u_ext, preferred_element_type=F32)
    later = sums[:, :n] + jnp.concatenate([c] * (n // LANES), axis=1)
    a = jnp.exp2((z2 - sp - later).astype(BF16))
    return sums[:, n:], a


SB_NORMAL, SB_DIAG, SB_SKIP = 0, 1, 2


def _sb_attn_kernel(mb_ref, q_ref, k_ref, v_ref, u_ref, o_ref, z_scr, acc_scr, c_scr, *, tq, n_heads, hd,
                    pairs_per_body):
    i = pl.program_id(1)
    heads_per_vreg = LANES // hd
    n_pairs = n_heads // heads_per_vreg
    lane = lax.broadcasted_iota(jnp.int32, (1, LANES), 1)
    u = u_ref[...]
    in_head = [(lane >= sub * hd) & (lane < (sub + 1) * hd) for sub in range(heads_per_vreg)]

    for grp in range(n_pairs // pairs_per_body):
        pairs = range(grp * pairs_per_body, (grp + 1) * pairs_per_body)
        qh = {}
        for pair in pairs:
            q128 = q_ref[:, pair * LANES:(pair + 1) * LANES]
            for sub in range(heads_per_vreg):
                qh[pair, sub] = jnp.where(in_head[sub], q128, jnp.zeros_like(q128))
                h = pair * heads_per_vreg + sub
                acc_scr[h] = jnp.zeros((tq, LANES), F32)
                c_scr[h] = jnp.zeros((tq, LANES), F32)

        def scores(kb, slot, sel):
            start = pl.multiple_of(jnp.maximum(kb, 0) * tq, tq)
            for pair in pairs:
                k128 = k_ref[pl.ds(start, tq), pair * LANES:(pair + 1) * LANES]
                for sub in range(heads_per_vreg):
                    h = pair * heads_per_vreg + sub
                    z_scr[h, slot] = _nt_dot(qh[pair, sub], k128) + mb_ref[h, sel]

        def weigh(kb, slot):
            start = pl.multiple_of(jnp.maximum(kb, 0) * tq, tq)
            for pair in pairs:
                v128 = v_ref[pl.ds(start, tq), pair * LANES:(pair + 1) * LANES]
                for sub in range(heads_per_vreg):
                    h = pair * heads_per_vreg + sub
                    c = c_scr[h]
                    rs, a = _sb_weights(z_scr[h, slot], u, c)
                    acc_scr[h] += jnp.dot(a, v128, preferred_element_type=F32)
                    c_scr[h] = c + rs

        def skip_if_negative(kb):
            return jnp.where(kb < 0, SB_SKIP, SB_NORMAL)

        scores(i, 0, SB_DIAG)

        def body(m, carry):
            kb = i - 2 * m
            scores(kb - 1, 1, skip_if_negative(kb - 1))
            weigh(kb, 0)
            scores(kb - 2, 0, skip_if_negative(kb - 2))
            weigh(kb - 1, 1)
            return carry

        lax.fori_loop(0, (i + 2) // 2, body, 0)
        for pair in pairs:
            out128 = acc_scr[pair * heads_per_vreg]
            for sub in range(1, heads_per_vreg):
                out128 = jnp.where(in_head[sub], acc_scr[pair * heads_per_vreg + sub], out128)
            o_ref[:, pair * LANES:(pair + 1) * LANES] = out128.astype(o_ref.dtype)


def _sb_attn(p, mb, u, *, batch, seq, tq, n_heads, hd, pairs_per_body=1):
    nq = seq // tq
    width = n_heads * hd
    assert width == GRP
    kern = functools.partial(_sb_attn_kernel, tq=tq, n_heads=n_heads, hd=hd, pairs_per_body=pairs_per_body)
    return pl.pallas_call(
        kern,
        grid=(batch, nq),
        in_specs=[
            pl.BlockSpec((n_heads, 3, tq, tq), lambda b, i: (0, 0, 0, 0)),
            pl.BlockSpec((tq, GRP), lambda b, i: (b * nq + i, J_QA)),
            pl.BlockSpec((seq, GRP), lambda b, i: (b, J_KA)),
            pl.BlockSpec((seq, GRP), lambda b, i: (b, J_VA)),
            pl.BlockSpec((tq, tq + LANES), lambda b, i: (0, 0)),
        ],
        out_specs=pl.BlockSpec((tq, GRP), lambda b, i: (b * nq + i, 0)),
        out_shape=jax.ShapeDtypeStruct((batch * seq, GRP), BF16),
        scratch_shapes=[pltpu.VMEM((n_heads, 2, tq, tq), F32), pltpu.VMEM((n_heads, tq, LANES), F32),
                        pltpu.VMEM((n_heads, tq, LANES), F32)],
        compiler_params=_cparams(("parallel", "arbitrary")),
        name="sb_attn",
    )(mb, p, p, p, u)


def _paged_attn_kernel(pt_ref, qbd_ref, bias_ref, mbnew_ref, knew_ref, vnew_ref, u_ref, unew_ref, ck_hbm, cv_hbm,
                       o_ref, kbuf, vbuf, sem, *, layer, n_groups, g_pages, page, n_heads, hd, n_tok):
    b = pl.program_id(0)
    nb = pl.num_programs(0)
    n_pages = n_groups * g_pages
    width = n_heads * hd

    def copies(bb, gi, slot):
        out = []
        for pi in range(g_pages):
            pg = pt_ref[bb, n_pages - (gi + 1) * g_pages + pi]
            out.append(pltpu.make_async_copy(ck_hbm.at[layer, pg], kbuf.at[slot, pi], sem.at[0, slot]))
            out.append(pltpu.make_async_copy(cv_hbm.at[layer, pg], vbuf.at[slot, pi], sem.at[1, slot]))
        return out

    @pl.when(b == 0)
    def _():
        for cp in copies(0, 0, 0):
            cp.start()

    qbd = qbd_ref[...]
    bias = bias_ref[...]
    rows = qbd.shape[0]

    rs, a = _sb_weights(jnp.dot(qbd, knew_ref[...], preferred_element_type=F32) + mbnew_ref[...], unew_ref[...],
                        jnp.zeros((rows, LANES), F32))
    c = rs
    acc = _nt_dot(a, vnew_ref[...])

    def body(gi, carry):
        n = b * n_groups + gi
        slot = n % 2
        last = gi == n_groups - 1

        @pl.when(jnp.logical_not(last))
        def _():
            for cp in copies(b, gi + 1, 1 - slot):
                cp.start()

        @pl.when(last & (b + 1 < nb))
        def _():
            for cp in copies(b + 1, 0, 1 - slot):
                cp.start()

        for cp in copies(b, gi, slot):
            cp.wait()
        c, acc = carry
        z2 = jnp.concatenate(
            [jnp.dot(qbd, kbuf[slot, pi].reshape(width, page).astype(BF16), preferred_element_type=F32)
             for pi in range(g_pages)], axis=1) + bias
        rs, a = _sb_weights(z2, u_ref[...], c)
        for pi in range(g_pages):
            acc = acc + _nt_dot(a[:, pi * page:(pi + 1) * page], vbuf[slot, pi].reshape(width, page).astype(BF16))
        return c + rs, acc

    c, acc = lax.fori_loop(0, n_groups, body, (c, acc))

    rr = lax.broadcasted_iota(jnp.int32, acc.shape, 0) % n_heads
    cc = lax.broadcasted_iota(jnp.int32, acc.shape, 1) // hd
    own = jnp.where(rr == cc, acc, 0.0)
    o_ref[...] = jnp.concatenate(
        [jnp.sum(own[t * n_heads:(t + 1) * n_heads], axis=0, keepdims=True) for t in range(n_tok)], axis=0)


def _paged_attn(page_table, qbd, bias_col, mb_new, knew_t, vnew_t, u_ext, u_new, cache_kt, cache_vt, *, layer,
                g_pages, n_tok):
    nb, n_pages = page_table.shape
    _, _, n_heads, hd, page = cache_kt.shape
    width = n_heads * hd
    rows = qbd.shape[1]
    gk = g_pages * page
    assert n_pages % g_pages == 0 and rows == n_tok * n_heads
    kern = functools.partial(_paged_attn_kernel, layer=layer, n_groups=n_pages // g_pages, g_pages=g_pages,
                             page=page, n_heads=n_heads, hd=hd, n_tok=n_tok)
    const2 = lambda b, pt: (0, 0)
    per_b = lambda b, pt: (b, 0, 0)
    grid_spec = pltpu.PrefetchScalarGridSpec(
        num_scalar_prefetch=1,
        grid=(nb,),
        in_specs=[
            pl.BlockSpec((None, rows, width), per_b),
            pl.BlockSpec((rows, 1), const2),
            pl.BlockSpec((rows, page), const2),
            pl.BlockSpec((None, width, page), per_b),
            pl.BlockSpec((None, width, page), per_b),
            pl.BlockSpec((gk, gk + LANES), const2),
            pl.BlockSpec((page, page + LANES), const2),
            pl.BlockSpec(memory_space=pl.ANY),
            pl.BlockSpec(memory_space=pl.ANY),
        ],
        out_specs=pl.BlockSpec((None, n_tok, width), per_b),
        scratch_shapes=[
            pltpu.VMEM((2, g_pages, n_heads, hd, page), F32),
            pltpu.VMEM((2, g_pages, n_heads, hd, page), F32),
            pltpu.SemaphoreType.DMA((2, 2)),
        ],
    )
    return pl.pallas_call(
        kern,
        grid_spec=grid_spec,
        out_shape=jax.ShapeDtypeStruct((nb, n_tok, width), F32),
        compiler_params=_cparams(("arbitrary",)),
        name="paged_attn",
    )(page_table, qbd, bias_col, mb_new, knew_t, vnew_t, u_ext, u_new, cache_kt, cache_vt)


def _mix_prompt_kernel(gpow_ref, ub_ref, vb_ref, qc_ref, kc_ref, vc_ref, gc_ref, wsp_ref, bsp_ref, decay_ref,
                       xi_ref, zeta_ref, ob_ref, oc_ref, sfin_ref, state, *, batch, n_grp, n_heads, t):
    ci = pl.program_id(0)

    @pl.when(ci == 0)
    def _():
        state[...] = jnp.zeros_like(state)

    r = lax.broadcasted_iota(jnp.int32, (t, t), 0)
    s = lax.broadcasted_iota(jnp.int32, (t, t), 1)
    tril = s <= r
    for b in range(batch):
        for g in range(n_grp):
            lanes = slice(g * LANES, (g + 1) * LANES)
            w = jnp.where(tril, wsp_ref[g], 0.0).astype(BF16)
            mixed = jnp.dot(w, vb_ref[b, :, lanes], preferred_element_type=F32) + bsp_ref[g]
            ob_ref[b, :, lanes] = (ub_ref[b, :, lanes].astype(F32) * mixed).astype(ob_ref.dtype)
        for h in range(n_heads):
            lanes = slice(h * LANES, (h + 1) * LANES)
            q = qc_ref[b, :, lanes]
            k = kc_ref[b, :, lanes]
            v = vc_ref[b, :, lanes]
            st = state[b, h]
            scores = _nt_dot(q, k) * decay_ref[h]
            inner = jnp.dot(scores.astype(BF16), v, preferred_element_type=F32)
            cross = jnp.dot(q, st.astype(BF16), preferred_element_type=F32) * xi_ref[h]
            kz = (k.astype(F32) * zeta_ref[h]).astype(BF16)
            upd = lax.dot_general(kz, v, (((0,), (0,)), ((), ())), preferred_element_type=F32)
            state[b, h] = gpow_ref[h] * st + upd
            on = _center_norm(inner + cross)
            oc_ref[b, :, lanes] = (on * gc_ref[b, :, lanes].astype(F32)).astype(oc_ref.dtype)

    @pl.when(ci == pl.num_programs(0) - 1)
    def _():
        sfin_ref[...] = state[...]


def _mix_prompt(p3, gpow, wsp_l, bsp_b, decay, xi_b, zeta_b, *, n_grp, n_heads):
    batch, seq, _ = p3.shape
    t = RET_CHUNK
    kern = functools.partial(_mix_prompt_kernel, batch=batch, n_grp=n_grp, n_heads=n_heads, t=t)
    col = lambda jj: (lambda c: (0, c, jj))
    const3 = lambda c: (0, 0, 0)
    return pl.pallas_call(
        kern,
        grid=(seq // t,),
        in_specs=[
            pl.BlockSpec(memory_space=pltpu.SMEM),
            pl.BlockSpec((batch, t, GRP), col(J_UB)),
            pl.BlockSpec((batch, t, GRP), col(J_VB)),
            pl.BlockSpec((batch, t, GRP), col(J_QC)),
            pl.BlockSpec((batch, t, GRP), col(J_KC)),
            pl.BlockSpec((batch, t, GRP), col(J_VC)),
            pl.BlockSpec((batch, t, GRP), col(J_GC)),
            pl.BlockSpec((n_grp, t, t), const3),
            pl.BlockSpec((n_grp, t, LANES), const3),
            pl.BlockSpec((n_heads, t, t), const3),
            pl.BlockSpec((n_heads, t, LANES), const3),
            pl.BlockSpec((n_heads, t, LANES), const3),
        ],
        out_specs=[
            pl.BlockSpec((batch, t, GRP), lambda c: (0, c, 0)),
            pl.BlockSpec((batch, t, GRP), lambda c: (0, c, 0)),
            pl.BlockSpec((batch, n_heads, LANES, LANES), lambda c: (0, 0, 0, 0)),
        ],
        out_shape=[
            jax.ShapeDtypeStruct((batch, seq, GRP), BF16),
            jax.ShapeDtypeStruct((batch, seq, GRP), BF16),
            jax.ShapeDtypeStruct((batch, n_heads, LANES, LANES), F32),
        ],
        scratch_shapes=[pltpu.VMEM((batch, n_heads, LANES, LANES), F32)],
        compiler_params=_cparams(("arbitrary",)),
        name="mix_prompt",
    )(gpow, p3, p3, p3, p3, p3, p3, wsp_l, bsp_b, decay, xi_b, zeta_b)


def _mix_sample_kernel(gpow_ref, ub_ref, vb_ref, qc_ref, kc_ref, vc_ref, gc_ref, mg_ref, bg_ref, dmat_ref,
                       xi_ref, zeta_ref, st_ref, ob_ref, oc_ref, snew_ref, *, n_batch, n_tok):
    h = pl.program_id(0)
    rows = n_batch * n_tok
    mixed = jnp.dot(mg_ref[...].astype(BF16), vb_ref[...].astype(BF16), preferred_element_type=F32) + bg_ref[...]
    ob_ref[...] = ub_ref[...] * mixed

    q = qc_ref[...].astype(BF16)
    k = kc_ref[...]
    v = vc_ref[...].astype(BF16)
    scores = _nt_dot(q, k.astype(BF16)) * dmat_ref[...]
    inner = jnp.dot(scores.astype(BF16), v, preferred_element_type=F32)
    kzt = (k * zeta_ref[...]).T
    gp = gpow_ref[h]
    row_b = lax.broadcasted_iota(jnp.int32, (rows, LANES), 0) // n_tok
    col_b = lax.broadcasted_iota(jnp.int32, (LANES, rows), 1) // n_tok

    def body(b, cross):
        st = st_ref[b]
        cr = jnp.dot(q, st.astype(BF16), preferred_element_type=F32)
        cross = jnp.where(row_b == b, cr, cross)
        upd = jnp.dot(jnp.where(col_b == b, kzt, 0.0).astype(BF16), v, preferred_element_type=F32)
        snew_ref[b] = gp * st + upd
        return cross

    cross = lax.fori_loop(0, n_batch, body, jnp.zeros((rows, LANES), F32))
    on = _center_norm(inner + cross * xi_ref[...])
    oc_ref[...] = on * gc_ref[...]


def _mix_sample(ps, state_ret, layer, gpow4, mg, bg, dmat, xi_s, zeta_s, *, n_batch, n_tok, n_heads):
    rows = n_batch * n_tok
    kern = functools.partial(_mix_sample_kernel, n_batch=n_batch, n_tok=n_tok)
    col = lambda jj: (lambda h: (0, jj * (GRP // LANES) + h))
    per_h = lambda h: (h, 0, 0)
    return pl.pallas_call(
        kern,
        grid=(n_heads,),
        in_specs=[
            pl.BlockSpec(memory_space=pltpu.SMEM),
            pl.BlockSpec((rows, LANES), col(J_UB)),
            pl.BlockSpec((rows, LANES), col(J_VB)),
            pl.BlockSpec((rows, LANES), col(J_QC)),
            pl.BlockSpec((rows, LANES), col(J_KC)),
            pl.BlockSpec((rows, LANES), col(J_VC)),
            pl.BlockSpec((rows, LANES), col(J_GC)),
            pl.BlockSpec((None, rows, rows), per_h),
            pl.BlockSpec((None, rows, LANES), per_h),
            pl.BlockSpec((None, rows, rows), per_h),
            pl.BlockSpec((None, rows, LANES), per_h),
            pl.BlockSpec((None, rows, LANES), per_h),
            pl.BlockSpec((None, n_batch, None, LANES, LANES), lambda h: (layer, 0, h, 0, 0)),
        ],
        out_specs=[
            pl.BlockSpec((rows, LANES), lambda h: (0, h)),
            pl.BlockSpec((rows, LANES), lambda h: (0, h)),
            pl.BlockSpec((n_batch, None, LANES, LANES), lambda h: (0, h, 0, 0)),
        ],
        out_shape=[
            jax.ShapeDtypeStruct((rows, GRP), F32),
            jax.ShapeDtypeStruct((rows, GRP), F32),
            jax.ShapeDtypeStruct((n_batch, n_heads, LANES, LANES), F32),
        ],
        compiler_params=_cparams(("parallel",)),
        name="mix_sample",
    )(gpow4, ps, ps, ps, ps, ps, ps, mg, bg, dmat, xi_s, zeta_s, state_ret)


def _merge_mlp_kernel(x_ref, gate1_ref, shift_ref, scale_ref, gate2_ref, g_ref, oa_ref, ob_ref, oc_ref, ga_ref,
                      gb_ref, gcm_ref, wa_ref, wb_ref, wc_ref, wo_ref, w1_ref, w2_ref, o_ref, h_scr, *, tf):
    def br(o_r, w_r, g_r):
        return g_r[...].astype(F32) * jnp.dot(o_r[...].astype(BF16), w_r[...], preferred_element_type=F32)

    merged = br(oa_ref, wa_ref, ga_ref) + br(ob_ref, wb_ref, gb_ref) + br(oc_ref, wc_ref, gcm_ref)
    x1 = x_ref[...] + gate1_ref[...] * jnp.dot(merged.astype(BF16), wo_ref[...], preferred_element_type=F32)
    o_ref[...] = x1
    ms = jnp.mean(x1 * x1, axis=-1, keepdims=True)
    y = x1 * lax.rsqrt(ms + EPS) * g_ref[...]
    h_scr[...] = (y * (1.0 + scale_ref[...]) + shift_ref[...]).astype(BF16)
    acc = None
    for j in range(w1_ref.shape[1] // tf):
        a = jnp.maximum(jnp.dot(h_scr[...], w1_ref[:, j * tf:(j + 1) * tf], preferred_element_type=F32), 0.0)
        t = jnp.dot((a * a).astype(BF16), w2_ref[j * tf:(j + 1) * tf, :], preferred_element_type=F32)
        acc = t if acc is None else acc + t
    o_ref[...] = o_ref[...] + gate2_ref[...] * acc


def _merge_mlp(x2d, mod4, layer, g_norm, p, oa, ob, oc, wa, wb, wc, wo, w1, w2, *, tm, tf, rows_per_mod):
    m, d = x2d.shape
    rm = mod4.shape[2]
    row = lambda i: (i, 0)
    mod_idx = (lambda chunk: (lambda i: (layer, i // rows_per_mod, 0, chunk)))
    wspec = lambda w: _resident((None,) + w.shape[1:], lambda i: (layer, 0, 0))
    return pl.pallas_call(
        functools.partial(_merge_mlp_kernel, tf=tf),
        grid=(m // tm,),
        in_specs=[
            pl.BlockSpec((tm, d), row),
            pl.BlockSpec((None, None, rm, d), mod_idx(2)),
            pl.BlockSpec((None, None, rm, d), mod_idx(3)),
            pl.BlockSpec((None, None, rm, d), mod_idx(4)),
            pl.BlockSpec((None, None, rm, d), mod_idx(5)),
            pl.BlockSpec((None, 1, d), lambda i: (layer, 0, 0)),
            pl.BlockSpec((tm, GRP), row),
            pl.BlockSpec((tm, GRP), row),
            pl.BlockSpec((tm, GRP), row),
            pl.BlockSpec((tm, d), lambda i: (i, J_GA // 2)),
            pl.BlockSpec((tm, d), lambda i: (i, J_GB // 2)),
            pl.BlockSpec((tm, d), lambda i: (i, J_GCM // 2)),
            wspec(wa), wspec(wb), wspec(wc), wspec(wo), wspec(w1), wspec(w2),
        ],
        out_specs=pl.BlockSpec((tm, d), row),
        out_shape=jax.ShapeDtypeStruct((m, d), F32),
        scratch_shapes=[pltpu.VMEM((tm, d), BF16)],
        compiler_params=_cparams(("parallel",)),
        name="merge_mlp",
    )(x2d, mod4, mod4, mod4, mod4, g_norm, oa, ob, oc, p, p, p, wa, wb, wc, wo, w1, w2)


def _rope_tables(pos, dk):
    half = dk // 2
    freqs = ROPE_BASE ** (-jnp.arange(half, dtype=F32) / half)
    ang = pos.astype(F32)[:, None] * freqs[None, :]
    cos, sin = jnp.cos(ang), jnp.sin(ang)
    return jnp.concatenate([cos, cos], axis=1), jnp.concatenate([-sin, sin], axis=1)


def kernel(x_prompt, x_sample, cache_k, cache_v, state_ret, page_table, c_prompt, c_sample, w_ada, b_ada, g_norm_mix, g_norm_mlp, w_in, g_qa, g_ka, sb_bias, g_vb, w_spatial, b_spatial, w_proj_a, w_proj_b, w_proj_c, w_out, w_mlp1, w_mlp2):
    batch, seq, d = x_prompt.shape
    nb, n_tok, _ = x_sample.shape
    depth, n_pool, page, n_heads_a, hd = cache_k.shape
    n_pages = page_table.shape[1]
    past = n_pages * page
    n_grp_b, chunk_b = w_spatial.shape[1], w_spatial.shape[2]
    w_b = g_vb.shape[1]
    n_heads_c, dk, dv = state_ret.shape[2:]
    w_a = n_heads_a * hd
    w_c = n_heads_c * dv
    rows_s = nb * n_tok
    assert w_a == GRP and w_b == GRP and w_c == GRP and n_heads_c * dk == GRP and d == 2 * GRP
    assert dk == LANES and dv == LANES and w_b // n_grp_b == LANES and chunk_b == RET_CHUNK and n_grp_b == n_heads_c
    assert seq % RET_CHUNK == 0 and n_tok <= chunk_b

    sizes = (w_a, w_a, w_a, w_b, w_b, w_c, w_c, w_c, w_c, d, d, d)
    offs = np.concatenate([[0], np.cumsum(sizes)])
    order = (9, 10, 11, 0, 1, 2, 3, 4, 5, 6, 7, 8)
    w_in_b = jnp.concatenate([w_in[:, :, offs[o]:offs[o + 1]] for o in order], axis=-1).astype(BF16)
    wa_b, wb_b, wc_b, wo_b = (w.astype(BF16) for w in (w_proj_a, w_proj_b, w_proj_c, w_out))
    w1_b, w2_b = w_mlp1.astype(BF16), w_mlp2.astype(BF16)
    g_mix = g_norm_mix.reshape(depth, 1, d)
    g_mlp = g_norm_mlp.reshape(depth, 1, d)
    g_qa_t = jnp.tile(g_qa, (1, n_heads_a)).reshape(depth, 1, GRP)
    g_ka_t = jnp.tile(g_ka, (1, n_heads_a)).reshape(depth, 1, GRP)
    g_vb_r = g_vb.reshape(depth, 1, GRP)
    gi = np.arange(GRP) // hd
    gsum = jnp.asarray((gi[:, None] == gi[None, :]).astype(np.float32) / hd, dtype=BF16)

    log_gamma = jnp.log1p(-jnp.exp2(-5.0 - jnp.arange(n_heads_c, dtype=F32)))

    def ret_tables(length):
        i = jnp.arange(length, dtype=F32)
        diff = i[:, None] - i[None, :]
        decay = jnp.where(diff >= 0, jnp.exp(log_gamma[:, None, None] * jnp.maximum(diff, 0.0)), 0.0)
        xi = jnp.exp(log_gamma[:, None] * (i[None, :] + 1.0))
        zeta = jnp.exp(log_gamma[:, None] * (length - 1.0 - i[None, :]))
        return decay, xi, zeta, jnp.exp(log_gamma * length)

    decay_p, xi_p, zeta_p, gpow_p = ret_tables(RET_CHUNK)
    xi_pb = jnp.broadcast_to(xi_p[:, :, None], (n_heads_c, RET_CHUNK, LANES))
    zeta_pb = jnp.broadcast_to(zeta_p[:, :, None], (n_heads_c, RET_CHUNK, LANES))
    decay_s, xi_s, zeta_s, gpow_s = ret_tables(n_tok)
    eye_b = jnp.eye(nb, dtype=F32)
    dmat_s = jnp.stack([jnp.kron(eye_b, decay_s[h]) for h in range(n_heads_c)])
    xi_sb = jnp.broadcast_to(jnp.tile(xi_s, (1, nb))[:, :, None], (n_heads_c, rows_s, LANES))
    zeta_sb = jnp.broadcast_to(jnp.tile(zeta_s, (1, nb))[:, :, None], (n_heads_c, rows_s, LANES))

    bsp_pb = jnp.broadcast_to(b_spatial[:, :, :, None], (depth, n_grp_b, chunk_b, LANES))
    w_sp_s = jnp.tril(w_spatial[:, :, :n_tok, :n_tok])
    mg_s = jnp.einsum("ab,lgts->lgatbs", eye_b, w_sp_s).reshape(depth, n_grp_b, rows_s, rows_s)
    bg_s = jnp.broadcast_to(jnp.tile(b_spatial[:, :, :n_tok], (1, 1, nb))[..., None],
                            (depth, n_grp_b, rows_s, LANES))

    cos_p, sin_p = _rope_tables(jnp.arange(seq), dk)
    cos_s, sin_s = _rope_tables(past + (jnp.arange(rows_s) % n_tok), dk)

    def later_sum_matrix(n):
        tri = np.arange(n)
        strict = (tri[:, None] > tri[None, :]).astype(np.float32)
        return jnp.asarray(np.concatenate([strict, np.ones((n, LANES), np.float32)], axis=1), dtype=BF16)

    tq = 256
    g_pages = 4
    assert n_tok <= page
    u_p = later_sum_matrix(tq)
    u_s = later_sum_matrix(g_pages * page)
    u_new = later_sum_matrix(page)

    m_all = batch + nb
    m_pad = -(-m_all // 8) * 8
    c_all = jnp.concatenate([c_prompt, c_sample, jnp.zeros((m_pad - m_all, d), F32)], axis=0)
    mod = _adaln(c_all, w_ada, b_ada)
    mod_p = mod[:, :batch].reshape(depth, batch, 1, 6 * d)
    mod_s = jnp.repeat(mod[:, batch:m_all], n_tok, axis=1).reshape(depth, 1, rows_s, 6 * d)

    sb_bias2 = sb_bias * LOG2E
    bias_full = jnp.broadcast_to(sb_bias2[:, :, None, None], (depth, n_heads_a, tq, tq))
    neg_inf = jnp.full_like(bias_full, -jnp.inf)
    strictly_causal = jnp.asarray(np.arange(tq)[None, :] < np.arange(tq)[:, None])
    mb_p = jnp.stack([bias_full, jnp.where(strictly_causal, bias_full, neg_inf), neg_inf], axis=2)
    q_head = np.arange(n_tok * n_heads_a) % n_heads_a
    q_tok = np.arange(n_tok * n_heads_a) // n_heads_a
    row_bias = sb_bias2[:, q_head]
    bias_col = row_bias[:, :, None]
    visible_new = jnp.asarray(np.arange(page)[None, :] < q_tok[:, None])
    mb_new = jnp.where(visible_new[None], bias_col, -jnp.inf)
    qbd_mask = jnp.asarray(q_head[:, None] == (np.arange(w_a) // hd)[None, :])
    cache_kt = jnp.transpose(cache_k, (0, 1, 3, 4, 2))
    cache_vt = jnp.transpose(cache_v, (0, 1, 3, 4, 2))

    tm_p = 512
    xp = x_prompt.reshape(batch * seq, d)
    xs = x_sample.reshape(rows_s, d)
    kp_l, vp_l, ks_l, vs_l, sp_l, ss_l, gv_l = [], [], [], [], [], [], []
    for l in range(depth):
        p, kp, vp = _inproj(xp, mod_p, l, g_mix, w_in_b, cos_p, sin_p, g_qa_t, g_ka_t, g_vb_r, gsum,
                            tm=tm_p, rows_per_mod=seq // tm_p, p_dtype=BF16, hd=hd, dk=dk, with_vb=False)
        oa = _sb_attn(p, mb_p[l], u_p, batch=batch, seq=seq, tq=tq, n_heads=n_heads_a, hd=hd, pairs_per_body=2)
        ob, oc, s_fin = _mix_prompt(p.reshape(batch, seq, -1), gpow_p, w_spatial[l], bsp_pb[l], decay_p, xi_pb,
                                    zeta_pb, n_grp=n_grp_b, n_heads=n_heads_c)
        xp = _merge_mlp(xp, mod_p, l, g_mlp, p, oa, ob.reshape(batch * seq, GRP), oc.reshape(batch * seq, GRP),
                        wa_b, wb_b, wc_b, wo_b, w1_b, w2_b, tm=tm_p, tf=1024, rows_per_mod=seq // tm_p)
        kp_l.append(kp)
        vp_l.append(vp)
        sp_l.append(s_fin)

        ps, ks, vs, vbs = _inproj(xs, mod_s, l, g_mix, w_in_b, cos_s, sin_s, g_qa_t, g_ka_t, g_vb_r, gsum,
                                  tm=rows_s, rows_per_mod=1, p_dtype=F32, hd=hd, dk=dk, with_vb=True)
        q_s = ps[:, J_QA * GRP:(J_QA + 1) * GRP].reshape(nb, n_tok, 1, w_a)
        qbd = jnp.where(qbd_mask, jnp.broadcast_to(q_s, (nb, n_tok, n_heads_a, w_a)).reshape(nb, -1, w_a), 0.0)
        pad = ((0, 0), (0, 0), (0, page - n_tok))
        knew_t = jnp.pad(jnp.transpose(ks.reshape(nb, n_tok, w_a), (0, 2, 1)), pad).astype(BF16)
        vnew_t = jnp.pad(jnp.transpose(vs.reshape(nb, n_tok, w_a), (0, 2, 1)), pad).astype(BF16)
        oas = _paged_attn(page_table, qbd.astype(BF16), bias_col[l], mb_new[l], knew_t, vnew_t, u_s, u_new,
                          cache_kt, cache_vt, layer=l, g_pages=g_pages, n_tok=n_tok)
        obs, ocs, s_new = _mix_sample(ps, state_ret, l, gpow_s, mg_s[l], bg_s[l], dmat_s, xi_sb, zeta_sb,
                                      n_batch=nb, n_tok=n_tok, n_heads=n_heads_c)
        xs = _merge_mlp(xs, mod_s, l, g_mlp, ps, oas.reshape(rows_s, w_a), obs, ocs, wa_b, wb_b, wc_b, wo_b,
                        w1_b, w2_b, tm=rows_s, tf=1024, rows_per_mod=1)
        ks_l.append(ks)
        vs_l.append(vs)
        ss_l.append(s_new)
        gv_l.append(vbs)

    def kv(lst, b_, l_):
        return jnp.stack(lst).reshape(depth, b_, l_, n_heads_a, hd)

    return (xp.reshape(batch, seq, d), xs.reshape(nb, n_tok, d),
            kv(kp_l, batch, seq), kv(vp_l, batch, seq), kv(ks_l, nb, n_tok), kv(vs_l, nb, n_tok),
            jnp.stack(sp_l), jnp.stack(ss_l), jnp.stack(gv_l).reshape(depth, nb, n_tok, w_b))
```

```python
import functools

import jax
import jax.numpy as jnp
import numpy as np
from jax import lax
from jax.experimental import pallas as pl
from jax.experimental.pallas import tpu as pltpu

F32 = jnp.float32
BF16 = jnp.bfloat16

EPS = 1e-6
LOG2E = float(np.log2(np.e))
ROPE_BASE = 10000.0
RET_CHUNK = 128
LANES = 128
V7X_VMEM_BYTES = 64 * 1024 * 1024
VMEM_LIMIT = 56 * 1024 * 1024

GRP = 512
J_GA, J_GB, J_GCM = 0, 2, 4
J_QA, J_KA, J_VA, J_UB, J_VB, J_QC, J_KC, J_VC, J_GC = 6, 7, 8, 9, 10, 11, 12, 13, 14
N_GRP = 15


def _cparams(sem, vmem=VMEM_LIMIT):
    return pltpu.CompilerParams(dimension_semantics=sem, vmem_limit_bytes=vmem)


def _resident(block_shape, index_map):
    return pl.BlockSpec(block_shape, index_map, pipeline_mode=pl.Buffered(1))


def _sigmoid(x):
    return 1.0 / (1.0 + jnp.exp(-x))


def _center_norm(o):
    mu = jnp.mean(o, axis=-1, keepdims=True)
    d = o - mu
    var = jnp.mean(d * d, axis=-1, keepdims=True)
    return d * lax.rsqrt(var + EPS)


def _gelu_tanh(x):
    return 0.5 * x * (1.0 + jnp.tanh(np.sqrt(2.0 / np.pi).astype(np.float32) * (x + 0.044715 * (x * x * x))))


def _nt_dot(a, b):
    return lax.dot_general(a, b, (((1,), (1,)), ((), ())), preferred_element_type=F32)


def _adaln_kernel(c_ref, w_ref, b_ref, o_ref):
    c = c_ref[...]
    s = (c * _sigmoid(c)).astype(BF16)
    o_ref[...] = jnp.dot(s, w_ref[...].astype(BF16), preferred_element_type=F32) + b_ref[...]


def _adaln(c_all, w_ada, b_ada, tn=1536):
    depth, d, n = w_ada.shape
    m = c_all.shape[0]
    return pl.pallas_call(
        _adaln_kernel,
        grid=(depth, n // tn),
        in_specs=[
            pl.BlockSpec((m, d), lambda l, j: (0, 0)),
            pl.BlockSpec((None, d, tn), lambda l, j: (l, 0, j)),
            pl.BlockSpec((None, 1, tn), lambda l, j: (l, 0, j)),
        ],
        out_specs=pl.BlockSpec((None, m, tn), lambda l, j: (l, 0, j)),
        out_shape=jax.ShapeDtypeStruct((depth, m, n), F32),
        compiler_params=_cparams(("parallel", "parallel")),
        name="adaln",
    )(c_all, w_ada, b_ada.reshape(depth, 1, n))


def _inproj_kernel(x_ref, shift_ref, scale_ref, g_ref, w_ref, cos_ref, sin_ref, gq_ref, gk_ref, gvb_ref,
                   gsum_ref, p_ref, k_ref, v_ref, *rest, hd_scale, kc_scale, with_vb, kv_transposed):
    vb_ref = rest[0] if with_vb else None
    h_scr = rest[-1]
    x = x_ref[...]
    ms = jnp.mean(x * x, axis=-1, keepdims=True)
    y = x * lax.rsqrt(ms + EPS) * g_ref[...]
    h_scr[...] = (y * (1.0 + scale_ref[...]) + shift_ref[...]).astype(BF16)

    def head_norm(a, g):
        ms = jnp.dot((a * a).astype(BF16), gsum_ref[...], preferred_element_type=F32)
        return a * lax.rsqrt(ms + EPS) * g

    def rotary(a):
        outs = []
        for h in range(GRP // LANES):
            xh = a[:, h * LANES:(h + 1) * LANES]
            outs.append(xh * cos_ref[...] + pltpu.roll(xh, LANES // 2, axis=1) * sin_ref[...])
        return jnp.concatenate(outs, axis=1)

    def activation(j, acc):
        if j < J_QA:
            return _sigmoid(acc)
        if j == J_QA:
            return head_norm(acc, gq_ref[...]) * hd_scale
        if j == J_KA:
            kn = head_norm(acc, gk_ref[...])
            k_ref[...] = kn.T if kv_transposed else kn
            return kn
        if j == J_VA:
            v_ref[...] = acc.T if kv_transposed else acc
            return acc
        if j == J_UB:
            return _gelu_tanh(acc)
        if j == J_VB:
            vb = _center_norm(_gelu_tanh(acc)) * gvb_ref[...]
            if with_vb:
                vb_ref[...] = vb
            return vb
        if j == J_QC:
            return rotary(acc)
        if j == J_KC:
            return rotary(acc) * kc_scale
        if j == J_VC:
            return acc
        assert j == J_GC
        return acc * _sigmoid(acc)

    for j in range(N_GRP):
        src = (j + N_GRP - J_QA) % N_GRP
        acc = jnp.dot(h_scr[...], w_ref[:, src * GRP:(src + 1) * GRP], preferred_element_type=F32)
        p_ref[:, j * GRP:(j + 1) * GRP] = activation(j, acc).astype(p_ref.dtype)


def _inproj(x2d, mod4, layer, g_norm, w_in_b, cos_t, sin_t, g_qa, g_ka, g_vb, gsum, *, tm, rows_per_mod,
            p_dtype, hd, dk, with_vb, kv_transposed):
    m, d = x2d.shape
    n_in = w_in_b.shape[-1]
    assert n_in == N_GRP * GRP and m % tm == 0
    rm = mod4.shape[2]
    seq = cos_t.shape[0]
    n_pos_blocks = seq // tm
    mod_idx = (lambda chunk: (lambda i: (layer, i // rows_per_mod, 0, chunk)))
    kern = functools.partial(_inproj_kernel, hd_scale=float(hd) ** -0.5 * LOG2E, kc_scale=float(dk) ** -0.5,
                             with_vb=with_vb, kv_transposed=kv_transposed)
    row_blk = lambda i: (i, 0)
    per_layer = lambda i: (layer, 0, 0)
    f32_specs = [pl.BlockSpec((tm, GRP), row_blk)] * (3 if with_vb else 2)
    f32_shapes = [jax.ShapeDtypeStruct((m, GRP), F32)] * (3 if with_vb else 2)
    if kv_transposed:
        f32_specs[:2] = [pl.BlockSpec((None, GRP, tm), lambda i: (i // n_pos_blocks, 0, i % n_pos_blocks))] * 2
        f32_shapes[:2] = [jax.ShapeDtypeStruct((m // seq, GRP, seq), F32)] * 2
    return pl.pallas_call(
        kern,
        grid=(m // tm,),
        in_specs=[
            pl.BlockSpec((tm, d), row_blk),
            pl.BlockSpec((None, None, rm, d), mod_idx(0)),
            pl.BlockSpec((None, None, rm, d), mod_idx(1)),
            pl.BlockSpec((None, 1, d), per_layer),
            _resident((None, d, n_in), per_layer),
            pl.BlockSpec((tm, LANES), lambda i: (i % n_pos_blocks, 0)),
            pl.BlockSpec((tm, LANES), lambda i: (i % n_pos_blocks, 0)),
            pl.BlockSpec((None, 1, GRP), per_layer),
            pl.BlockSpec((None, 1, GRP), per_layer),
            pl.BlockSpec((None, 1, GRP), per_layer),
            _resident((GRP, GRP), lambda i: (0, 0)),
        ],
        out_specs=[pl.BlockSpec((tm, n_in), row_blk)] + f32_specs,
        out_shape=[jax.ShapeDtypeStruct((m, n_in), p_dtype)] + f32_shapes,
        scratch_shapes=[pltpu.VMEM((tm, d), BF16)],
        compiler_params=_cparams(("parallel",)),
        name="inproj",
    )(x2d, mod4, mod4, g_norm, w_in_b, cos_t, sin_t, g_qa, g_ka, g_vb, gsum)


def _neg_abs(x):
    return pltpu.bitcast(pltpu.bitcast(x, jnp.uint32) | jnp.uint32(0x80000000), F32)


def _sb_weights(z2, u, c):
    n = z2.shape[1]
    sp = jnp.maximum(z2, 0.0) + jnp.log2(1.0 + jnp.exp2(_neg_abs(z2)))
    sums = jnp.dot(sp.astype(BF16), u, preferred_element_type=F32)
    later = sums + jnp.concatenate([c] * (n // LANES), axis=1)
    a = jnp.exp2(z2 - sp - later).astype(BF16)
    return jnp.broadcast_to(sums[:, :1] + sp[:, :1], c.shape), a


SB_NORMAL, SB_DIAG, SB_SKIP = 0, 1, 2


def _sb_attn_kernel(mb_ref, q_ref, k_ref, v_ref, u_ref, o_ref, z_scr, acc_scr, c_scr, *, tq, n_heads, hd,
                    pairs_per_body):
    i = pl.program_id(1)
    heads_per_vreg = LANES // hd
    n_pairs = n_heads // heads_per_vreg
    lane = lax.broadcasted_iota(jnp.int32, (1, LANES), 1)
    u = u_ref[...]
    in_head = [(lane >= sub * hd) & (lane < (sub + 1) * hd) for sub in range(heads_per_vreg)]

    for grp in range(n_pairs // pairs_per_body):
        pairs = range(grp * pairs_per_body, (grp + 1) * pairs_per_body)
        qh = {}
        for pair in pairs:
            q128 = q_ref[:, pair * LANES:(pair + 1) * LANES]
            for sub in range(heads_per_vreg):
                qh[pair, sub] = jnp.where(in_head[sub], q128, jnp.zeros_like(q128))
                h = pair * heads_per_vreg + sub
                acc_scr[h] = jnp.zeros((tq, LANES), F32)
                c_scr[h] = jnp.zeros((tq, LANES), F32)

        def scores(kb, slot, sel):
            start = pl.multiple_of(jnp.maximum(kb, 0) * tq, tq)
            for pair in pairs:
                k128 = k_ref[pl.ds(start, tq), pair * LANES:(pair + 1) * LANES]
                for sub in range(heads_per_vreg):
                    h = pair * heads_per_vreg + sub
                    z_scr[h, slot] = _nt_dot(qh[pair, sub], k128) + mb_ref[h, sel]

        def weigh(kb, slot):
            start = pl.multiple_of(jnp.maximum(kb, 0) * tq, tq)
            for pair in pairs:
                v128 = v_ref[pl.ds(start, tq), pair * LANES:(pair + 1) * LANES]
                for sub in range(heads_per_vreg):
                    h = pair * heads_per_vreg + sub
                    c = c_scr[h]
                    rs, a = _sb_weights(z_scr[h, slot], u, c)
                    acc_scr[h] += jnp.dot(a, v128, preferred_element_type=F32)
                    c_scr[h] = c + rs

        def skip_if_negative(kb):
            return jnp.where(kb < 0, SB_SKIP, SB_NORMAL)

        scores(i, 0, SB_DIAG)

        def body(m, carry):
            kb = i - 2 * m
            scores(kb - 1, 1, skip_if_negative(kb - 1))
            weigh(kb, 0)
            scores(kb - 2, 0, skip_if_negative(kb - 2))
            weigh(kb - 1, 1)
            return carry

        lax.fori_loop(0, (i + 2) // 2, body, 0)
        for pair in pairs:
            out128 = acc_scr[pair * heads_per_vreg]
            for sub in range(1, heads_per_vreg):
                out128 = jnp.where(in_head[sub], acc_scr[pair * heads_per_vreg + sub], out128)
            o_ref[:, pair * LANES:(pair + 1) * LANES] = out128.astype(o_ref.dtype)


def _sb_attn(p, mb, u, *, batch, seq, tq, n_heads, hd, pairs_per_body=1):
    nq = seq // tq
    width = n_heads * hd
    assert width == GRP
    kern = functools.partial(_sb_attn_kernel, tq=tq, n_heads=n_heads, hd=hd, pairs_per_body=pairs_per_body)
    return pl.pallas_call(
        kern,
        grid=(batch, nq),
        in_specs=[
            pl.BlockSpec((n_heads, 3, tq, tq), lambda b, i: (0, 0, 0, 0)),
            pl.BlockSpec((tq, GRP), lambda b, i: (b * nq + i, J_QA)),
            pl.BlockSpec((seq, GRP), lambda b, i: (b, J_KA)),
            pl.BlockSpec((seq, GRP), lambda b, i: (b, J_VA)),
            pl.BlockSpec((tq, tq), lambda b, i: (0, 0)),
        ],
        out_specs=pl.BlockSpec((tq, GRP), lambda b, i: (b * nq + i, 0)),
        out_shape=jax.ShapeDtypeStruct((batch * seq, GRP), BF16),
        scratch_shapes=[pltpu.VMEM((n_heads, 2, tq, tq), F32), pltpu.VMEM((n_heads, tq, LANES), F32),
                        pltpu.VMEM((n_heads, tq, LANES), F32)],
        compiler_params=_cparams(("parallel", "arbitrary")),
        name="sb_attn",
    )(mb, p, p, p, u)


def _paged_attn_kernel(pt_ref, qbd_ref, bias_ref, mbnew_ref, knew_ref, vnew_ref, u_ref, unew_ref, ck_hbm, cv_hbm,
                       o_ref, kbuf, vbuf, sem, *, layer, n_groups, g_pages, page, n_heads, hd, n_tok):
    b = pl.program_id(0)
    n_pages = n_groups * g_pages
    width = n_heads * hd
    n_slots = kbuf.shape[0]
    total = pl.num_programs(0) * n_groups

    def copies(g, slot):
        bb = g // n_groups
        gi = g % n_groups
        out = []
        for pi in range(g_pages):
            pg = pt_ref[bb, n_pages - (gi + 1) * g_pages + pi]
            out.append((pltpu.make_async_copy(ck_hbm.at[layer, pg], kbuf.at[slot, pi], sem.at[0, slot]),
                        pltpu.make_async_copy(cv_hbm.at[layer, pg], vbuf.at[slot, pi], sem.at[1, slot])))
        return out

    def start(g):
        for ck, cv in copies(g, g % n_slots):
            ck.start(priority=0)
            cv.start(priority=1)

    @pl.when(b == 0)
    def _():
        for g in range(n_slots - 1):
            start(g)

    qbd = qbd_ref[...]
    bias = bias_ref[...]
    rows = qbd.shape[0]

    rs, a = _sb_weights(jnp.dot(qbd, knew_ref[...], preferred_element_type=F32) + mbnew_ref[...], unew_ref[...],
                        jnp.zeros((rows, LANES), F32))
    c = rs
    acc = _nt_dot(a, vnew_ref[...])

    def body(gi, carry):
        g = b * n_groups + gi
        slot = g % n_slots

        @pl.when(g + n_slots - 1 < total)
        def _():
            start(g + n_slots - 1)

        for ck, cv in copies(g, slot):
            ck.wait()
            cv.wait()
        c, acc = carry
        z2 = jnp.concatenate(
            [jnp.dot(qbd, kbuf[slot, pi].reshape(width, page).astype(BF16), preferred_element_type=F32)
             for pi in range(g_pages)], axis=1) + bias
        rs, a = _sb_weights(z2, u_ref[...], c)
        for pi in range(g_pages):
            acc = acc + _nt_dot(a[:, pi * page:(pi + 1) * page], vbuf[slot, pi].reshape(width, page).astype(BF16))
        return c + rs, acc

    c, acc = lax.fori_loop(0, n_groups, body, (c, acc))

    rr = lax.broadcasted_iota(jnp.int32, acc.shape, 0) % n_heads
    cc = lax.broadcasted_iota(jnp.int32, acc.shape, 1) // hd
    own = jnp.where(rr == cc, acc, 0.0)
    o_ref[...] = jnp.concatenate(
        [jnp.sum(own[t * n_heads:(t + 1) * n_heads], axis=0, keepdims=True) for t in range(n_tok)], axis=0)


def _paged_attn(page_table, qbd, bias_col, mb_new, knew_t, vnew_t, u, u_new, cache_kt, cache_vt, *, layer,
                g_pages, n_slots, n_tok):
    nb, n_pages = page_table.shape
    _, _, n_heads, hd, page = cache_kt.shape
    width = n_heads * hd
    rows = qbd.shape[1]
    gk = g_pages * page
    assert n_pages % g_pages == 0 and rows == n_tok * n_heads
    kern = functools.partial(_paged_attn_kernel, layer=layer, n_groups=n_pages // g_pages, g_pages=g_pages,
                             page=page, n_heads=n_heads, hd=hd, n_tok=n_tok)
    const2 = lambda b, pt: (0, 0)
    per_b = lambda b, pt: (b, 0, 0)
    grid_spec = pltpu.PrefetchScalarGridSpec(
        num_scalar_prefetch=1,
        grid=(nb,),
        in_specs=[
            pl.BlockSpec((None, rows, width), per_b),
            pl.BlockSpec((rows, 1), const2),
            pl.BlockSpec((rows, page), const2),
            pl.BlockSpec((None, width, page), per_b),
            pl.BlockSpec((None, width, page), per_b),
            pl.BlockSpec((gk, gk), const2),
            pl.BlockSpec((page, page), const2),
            pl.BlockSpec(memory_space=pl.ANY),
            pl.BlockSpec(memory_space=pl.ANY),
        ],
        out_specs=pl.BlockSpec((None, n_tok, width), per_b),
        scratch_shapes=[
            pltpu.VMEM((n_slots, g_pages, n_heads, hd, page), F32),
            pltpu.VMEM((n_slots, g_pages, n_heads, hd, page), F32),
            pltpu.SemaphoreType.DMA((2, n_slots)),
        ],
    )
    return pl.pallas_call(
        kern,
        grid_spec=grid_spec,
        out_shape=jax.ShapeDtypeStruct((nb, n_tok, width), F32),
        compiler_params=_cparams(("arbitrary",)),
        name="paged_attn",
    )(page_table, qbd, bias_col, mb_new, knew_t, vnew_t, u, u_new, cache_kt, cache_vt)


def _mix_prompt_kernel(gpow_ref, ub_ref, vb_ref, qc_ref, kc_ref, vc_ref, gc_ref, wsp_ref, bsp_ref, decay_ref,
                       xi_ref, zeta_ref, ob_ref, oc_ref, sfin_ref, state, *, batch, n_grp, n_heads, t):
    ci = pl.program_id(0)

    @pl.when(ci == 0)
    def _():
        state[...] = jnp.zeros_like(state)

    r = lax.broadcasted_iota(jnp.int32, (t, t), 0)
    s = lax.broadcasted_iota(jnp.int32, (t, t), 1)
    tril = s <= r
    for b in range(batch):
        for g in range(n_grp):
            lanes = slice(g * LANES, (g + 1) * LANES)
            w = jnp.where(tril, wsp_ref[g], 0.0).astype(BF16)
            mixed = jnp.dot(w, vb_ref[b, :, lanes], preferred_element_type=F32) + bsp_ref[g]
            ob_ref[b, :, lanes] = (ub_ref[b, :, lanes].astype(F32) * mixed).astype(ob_ref.dtype)
        for h in range(n_heads):
            lanes = slice(h * LANES, (h + 1) * LANES)
            q = qc_ref[b, :, lanes]
            k = kc_ref[b, :, lanes]
            v = vc_ref[b, :, lanes]
            st = state[b, h]
            scores = _nt_dot(q, k) * decay_ref[h]
            inner = jnp.dot(scores.astype(BF16), v, preferred_element_type=F32)
            cross = jnp.dot(q, st.astype(BF16), preferred_element_type=F32) * xi_ref[h]
            kz = (k.astype(F32) * zeta_ref[h]).astype(BF16)
            upd = lax.dot_general(kz, v, (((0,), (0,)), ((), ())), preferred_element_type=F32)
            state[b, h] = gpow_ref[h] * st + upd
            on = _center_norm(inner + cross)
            oc_ref[b, :, lanes] = (on * gc_ref[b, :, lanes].astype(F32)).astype(oc_ref.dtype)

    @pl.when(ci == pl.num_programs(0) - 1)
    def _():
        sfin_ref[...] = state[...]


def _mix_prompt(p3, gpow, wsp_l, bsp_b, decay, xi_b, zeta_b, *, n_grp, n_heads):
    batch, seq, _ = p3.shape
    t = RET_CHUNK
    kern = functools.partial(_mix_prompt_kernel, batch=batch, n_grp=n_grp, n_heads=n_heads, t=t)
    col = lambda jj: (lambda c: (0, c, jj))
    const3 = lambda c: (0, 0, 0)
    return pl.pallas_call(
        kern,
        grid=(seq // t,),
        in_specs=[
            pl.BlockSpec(memory_space=pltpu.SMEM),
            pl.BlockSpec((batch, t, GRP), col(J_UB)),
            pl.BlockSpec((batch, t, GRP), col(J_VB)),
            pl.BlockSpec((batch, t, GRP), col(J_QC)),
            pl.BlockSpec((batch, t, GRP), col(J_KC)),
            pl.BlockSpec((batch, t, GRP), col(J_VC)),
            pl.BlockSpec((batch, t, GRP), col(J_GC)),
            pl.BlockSpec((n_grp, t, t), const3),
            pl.BlockSpec((n_grp, t, LANES), const3),
            pl.BlockSpec((n_heads, t, t), const3),
            pl.BlockSpec((n_heads, t, LANES), const3),
            pl.BlockSpec((n_heads, t, LANES), const3),
        ],
        out_specs=[
            pl.BlockSpec((batch, t, GRP), lambda c: (0, c, 0)),
            pl.BlockSpec((batch, t, GRP), lambda c: (0, c, 0)),
            pl.BlockSpec((batch, n_heads, LANES, LANES), lambda c: (0, 0, 0, 0)),
        ],
        out_shape=[
            jax.ShapeDtypeStruct((batch, seq, GRP), BF16),
            jax.ShapeDtypeStruct((batch, seq, GRP), BF16),
            jax.ShapeDtypeStruct((batch, n_heads, LANES, LANES), F32),
        ],
        scratch_shapes=[pltpu.VMEM((batch, n_heads, LANES, LANES), F32)],
        compiler_params=_cparams(("arbitrary",)),
        name="mix_prompt",
    )(gpow, p3, p3, p3, p3, p3, p3, wsp_l, bsp_b, decay, xi_b, zeta_b)


def _mix_sample_kernel(gpow_ref, ub_ref, vb_ref, qc_ref, kc_ref, vc_ref, gc_ref, mg_ref, bg_ref, dmat_ref,
                       xi_ref, zeta_ref, st_ref, ob_ref, oc_ref, snew_ref, *, n_batch, n_tok):
    h = pl.program_id(0)
    rows = n_batch * n_tok
    mixed = jnp.dot(mg_ref[...].astype(BF16), vb_ref[...].astype(BF16), preferred_element_type=F32) + bg_ref[...]
    ob_ref[...] = ub_ref[...] * mixed

    q = qc_ref[...].astype(BF16)
    k = kc_ref[...]
    v = vc_ref[...].astype(BF16)
    scores = _nt_dot(q, k.astype(BF16)) * dmat_ref[...]
    inner = jnp.dot(scores.astype(BF16), v, preferred_element_type=F32)
    kzt = (k * zeta_ref[...]).T
    gp = gpow_ref[h]
    row_b = lax.broadcasted_iota(jnp.int32, (rows, LANES), 0) // n_tok
    col_b = lax.broadcasted_iota(jnp.int32, (LANES, rows), 1) // n_tok

    def body(b, cross):
        st = st_ref[b]
        cr = jnp.dot(q, st.astype(BF16), preferred_element_type=F32)
        cross = jnp.where(row_b == b, cr, cross)
        upd = jnp.dot(jnp.where(col_b == b, kzt, 0.0).astype(BF16), v, preferred_element_type=F32)
        snew_ref[b] = gp * st + upd
        return cross

    cross = lax.fori_loop(0, n_batch, body, jnp.zeros((rows, LANES), F32))
    on = _center_norm(inner + cross * xi_ref[...])
    oc_ref[...] = on * gc_ref[...]


def _mix_sample(ps, state_ret, layer, gpow4, mg, bg, dmat, xi_s, zeta_s, *, n_batch, n_tok, n_heads):
    rows = n_batch * n_tok
    kern = functools.partial(_mix_sample_kernel, n_batch=n_batch, n_tok=n_tok)
    col = lambda jj: (lambda h: (0, jj * (GRP // LANES) + h))
    per_h = lambda h: (h, 0, 0)
    return pl.pallas_call(
        kern,
        grid=(n_heads,),
        in_specs=[
            pl.BlockSpec(memory_space=pltpu.SMEM),
            pl.BlockSpec((rows, LANES), col(J_UB)),
            pl.BlockSpec((rows, LANES), col(J_VB)),
            pl.BlockSpec((rows, LANES), col(J_QC)),
            pl.BlockSpec((rows, LANES), col(J_KC)),
            pl.BlockSpec((rows, LANES), col(J_VC)),
            pl.BlockSpec((rows, LANES), col(J_GC)),
            pl.BlockSpec((None, rows, rows), per_h),
            pl.BlockSpec((None, rows, LANES), per_h),
            pl.BlockSpec((None, rows, rows), per_h),
            pl.BlockSpec((None, rows, LANES), per_h),
            pl.BlockSpec((None, rows, LANES), per_h),
            pl.BlockSpec((None, n_batch, None, LANES, LANES), lambda h: (layer, 0, h, 0, 0)),
        ],
        out_specs=[
            pl.BlockSpec((rows, LANES), lambda h: (0, h)),
            pl.BlockSpec((rows, LANES), lambda h: (0, h)),
            pl.BlockSpec((n_batch, None, LANES, LANES), lambda h: (0, h, 0, 0)),
        ],
        out_shape=[
            jax.ShapeDtypeStruct((rows, GRP), F32),
            jax.ShapeDtypeStruct((rows, GRP), F32),
            jax.ShapeDtypeStruct((n_batch, n_heads, LANES, LANES), F32),
        ],
        compiler_params=_cparams(("parallel",)),
        name="mix_sample",
    )(gpow4, ps, ps, ps, ps, ps, ps, mg, bg, dmat, xi_s, zeta_s, state_ret)


def _merge_mlp_kernel(x_ref, gate1_ref, shift_ref, scale_ref, gate2_ref, g_ref, oa_ref, ob_ref, oc_ref, ga_ref,
                      gb_ref, gcm_ref, wa_ref, wb_ref, wc_ref, wo_ref, w1_ref, w2_ref, o_ref, h_scr, *, tf):
    def br(o_r, w_r, g_r):
        return g_r[...].astype(F32) * jnp.dot(o_r[...].astype(BF16), w_r[...], preferred_element_type=F32)

    merged = br(oa_ref, wa_ref, ga_ref) + br(ob_ref, wb_ref, gb_ref) + br(oc_ref, wc_ref, gcm_ref)
    x1 = x_ref[...] + gate1_ref[...] * jnp.dot(merged.astype(BF16), wo_ref[...], preferred_element_type=F32)
    o_ref[...] = x1
    ms = jnp.mean(x1 * x1, axis=-1, keepdims=True)
    y = x1 * lax.rsqrt(ms + EPS) * g_ref[...]
    h_scr[...] = (y * (1.0 + scale_ref[...]) + shift_ref[...]).astype(BF16)
    acc = None
    for j in range(w1_ref.shape[1] // tf):
        a = jnp.maximum(jnp.dot(h_scr[...], w1_ref[:, j * tf:(j + 1) * tf], preferred_element_type=F32), 0.0)
        t = jnp.dot((a * a).astype(BF16), w2_ref[j * tf:(j + 1) * tf, :], preferred_element_type=F32)
        acc = t if acc is None else acc + t
    o_ref[...] = o_ref[...] + gate2_ref[...] * acc


def _merge_mlp(x2d, mod4, layer, g_norm, p, oa, ob, oc, wa, wb, wc, wo, w1, w2, *, tm, tf, rows_per_mod):
    m, d = x2d.shape
    rm = mod4.shape[2]
    row = lambda i: (i, 0)
    mod_idx = (lambda chunk: (lambda i: (layer, i // rows_per_mod, 0, chunk)))
    wspec = lambda w: _resident((None,) + w.shape[1:], lambda i: (layer, 0, 0))
    return pl.pallas_call(
        functools.partial(_merge_mlp_kernel, tf=tf),
        grid=(m // tm,),
        in_specs=[
            pl.BlockSpec((tm, d), row),
            pl.BlockSpec((None, None, rm, d), mod_idx(2)),
            pl.BlockSpec((None, None, rm, d), mod_idx(3)),
            pl.BlockSpec((None, None, rm, d), mod_idx(4)),
            pl.BlockSpec((None, None, rm, d), mod_idx(5)),
            pl.BlockSpec((None, 1, d), lambda i: (layer, 0, 0)),
            pl.BlockSpec((tm, GRP), row),
            pl.BlockSpec((tm, GRP), row),
            pl.BlockSpec((tm, GRP), row),
            pl.BlockSpec((tm, d), lambda i: (i, J_GA // 2)),
            pl.BlockSpec((tm, d), lambda i: (i, J_GB // 2)),
            pl.BlockSpec((tm, d), lambda i: (i, J_GCM // 2)),
            wspec(wa), wspec(wb), wspec(wc), wspec(wo), wspec(w1), wspec(w2),
        ],
        out_specs=pl.BlockSpec((tm, d), row),
        out_shape=jax.ShapeDtypeStruct((m, d), F32),
        scratch_shapes=[pltpu.VMEM((tm, d), BF16)],
        compiler_params=_cparams(("parallel",)),
        name="merge_mlp",
    )(x2d, mod4, mod4, mod4, mod4, g_norm, oa, ob, oc, p, p, p, wa, wb, wc, wo, w1, w2)


def _rope_tables(pos, dk):
    half = dk // 2
    freqs = ROPE_BASE ** (-jnp.arange(half, dtype=F32) / half)
    ang = pos.astype(F32)[:, None] * freqs[None, :]
    cos, sin = jnp.cos(ang), jnp.sin(ang)
    return jnp.concatenate([cos, cos], axis=1), jnp.concatenate([-sin, sin], axis=1)


def kernel(x_prompt, x_sample, cache_k, cache_v, state_ret, page_table, c_prompt, c_sample, w_ada, b_ada, g_norm_mix, g_norm_mlp, w_in, g_qa, g_ka, sb_bias, g_vb, w_spatial, b_spatial, w_proj_a, w_proj_b, w_proj_c, w_out, w_mlp1, w_mlp2):
    batch, seq, d = x_prompt.shape
    nb, n_tok, _ = x_sample.shape
    depth, n_pool, page, n_heads_a, hd = cache_k.shape
    n_pages = page_table.shape[1]
    past = n_pages * page
    n_grp_b, chunk_b = w_spatial.shape[1], w_spatial.shape[2]
    w_b = g_vb.shape[1]
    n_heads_c, dk, dv = state_ret.shape[2:]
    w_a = n_heads_a * hd
    w_c = n_heads_c * dv
    rows_s = nb * n_tok
    assert w_a == GRP and w_b == GRP and w_c == GRP and n_heads_c * dk == GRP and d == 2 * GRP
    assert dk == LANES and dv == LANES and w_b // n_grp_b == LANES and chunk_b == RET_CHUNK and n_grp_b == n_heads_c
    assert seq % RET_CHUNK == 0 and n_tok <= chunk_b

    w_in_b = w_in.astype(BF16)
    wa_b, wb_b, wc_b, wo_b = (w.astype(BF16) for w in (w_proj_a, w_proj_b, w_proj_c, w_out))
    w1_b, w2_b = w_mlp1.astype(BF16), w_mlp2.astype(BF16)
    g_mix = g_norm_mix.reshape(depth, 1, d)
    g_mlp = g_norm_mlp.reshape(depth, 1, d)
    g_qa_t = jnp.tile(g_qa, (1, n_heads_a)).reshape(depth, 1, GRP)
    g_ka_t = jnp.tile(g_ka, (1, n_heads_a)).reshape(depth, 1, GRP)
    g_vb_r = g_vb.reshape(depth, 1, GRP)
    gi = np.arange(GRP) // hd
    gsum = jnp.asarray((gi[:, None] == gi[None, :]).astype(np.float32) / hd, dtype=BF16)

    log_gamma = jnp.log1p(-jnp.exp2(-5.0 - jnp.arange(n_heads_c, dtype=F32)))

    def ret_tables(length):
        i = jnp.arange(length, dtype=F32)
        diff = i[:, None] - i[None, :]
        decay = jnp.where(diff >= 0, jnp.exp(log_gamma[:, None, None] * jnp.maximum(diff, 0.0)), 0.0)
        xi = jnp.exp(log_gamma[:, None] * (i[None, :] + 1.0))
        zeta = jnp.exp(log_gamma[:, None] * (length - 1.0 - i[None, :]))
        return decay, xi, zeta, jnp.exp(log_gamma * length)

    decay_p, xi_p, zeta_p, gpow_p = ret_tables(RET_CHUNK)
    xi_pb = jnp.broadcast_to(xi_p[:, :, None], (n_heads_c, RET_CHUNK, LANES))
    zeta_pb = jnp.broadcast_to(zeta_p[:, :, None], (n_heads_c, RET_CHUNK, LANES))
    decay_s, xi_s, zeta_s, gpow_s = ret_tables(n_tok)
    eye_b = jnp.eye(nb, dtype=F32)
    dmat_s = jnp.stack([jnp.kron(eye_b, decay_s[h]) for h in range(n_heads_c)])
    xi_sb = jnp.broadcast_to(jnp.tile(xi_s, (1, nb))[:, :, None], (n_heads_c, rows_s, LANES))
    zeta_sb = jnp.broadcast_to(jnp.tile(zeta_s, (1, nb))[:, :, None], (n_heads_c, rows_s, LANES))

    bsp_pb = jnp.broadcast_to(b_spatial[:, :, :, None], (depth, n_grp_b, chunk_b, LANES))
    w_sp_s = jnp.tril(w_spatial[:, :, :n_tok, :n_tok])
    mg_s = jnp.einsum("ab,lgts->lgatbs", eye_b, w_sp_s).reshape(depth, n_grp_b, rows_s, rows_s)
    bg_s = jnp.broadcast_to(jnp.tile(b_spatial[:, :, :n_tok], (1, 1, nb))[..., None],
                            (depth, n_grp_b, rows_s, LANES))

    cos_p, sin_p = _rope_tables(jnp.arange(seq), dk)
    cos_s, sin_s = _rope_tables(past + (jnp.arange(rows_s) % n_tok), dk)

    def later_sum_matrix(n):
        tri = np.arange(n)
        return jnp.asarray((tri[:, None] > tri[None, :]).astype(np.float32), dtype=BF16)

    tq = 256
    g_pages = 4
    assert n_tok <= page
    u_p = later_sum_matrix(tq)
    u_s = later_sum_matrix(g_pages * page)
    u_new = later_sum_matrix(page)

    m_all = batch + nb
    m_pad = -(-m_all // 8) * 8
    c_all = jnp.concatenate([c_prompt, c_sample, jnp.zeros((m_pad - m_all, d), F32)], axis=0)
    mod = _adaln(c_all, w_ada, b_ada)
    mod_p = mod[:, :batch].reshape(depth, batch, 1, 6 * d)
    mod_s = jnp.repeat(mod[:, batch:m_all], n_tok, axis=1).reshape(depth, 1, rows_s, 6 * d)

    sb_bias2 = sb_bias * LOG2E
    bias_full = jnp.broadcast_to(sb_bias2[:, :, None, None], (depth, n_heads_a, tq, tq))
    neg_inf = jnp.full_like(bias_full, -jnp.inf)
    strictly_causal = jnp.asarray(np.arange(tq)[None, :] < np.arange(tq)[:, None])
    mb_p = jnp.stack([bias_full, jnp.where(strictly_causal, bias_full, neg_inf), neg_inf], axis=2)
    q_head = np.arange(n_tok * n_heads_a) % n_heads_a
    q_tok = np.arange(n_tok * n_heads_a) // n_heads_a
    row_bias = sb_bias2[:, q_head]
    bias_col = row_bias[:, :, None]
    visible_new = jnp.asarray(np.arange(page)[None, :] < q_tok[:, None])
    mb_new = jnp.where(visible_new[None], bias_col, -jnp.inf)
    qbd_mask = jnp.asarray(q_head[:, None] == (np.arange(w_a) // hd)[None, :])
    cache_kt = jnp.transpose(cache_k, (0, 1, 3, 4, 2))
    cache_vt = jnp.transpose(cache_v, (0, 1, 3, 4, 2))

    tm_p = 512
    xp = x_prompt.reshape(batch * seq, d)
    xs = x_sample.reshape(rows_s, d)
    kp_l, vp_l, ks_l, vs_l, sp_l, ss_l, gv_l = [], [], [], [], [], [], []
    for l in range(depth):
        p, kp, vp = _inproj(xp, mod_p, l, g_mix, w_in_b, cos_p, sin_p, g_qa_t, g_ka_t, g_vb_r, gsum,
                            tm=tm_p, rows_per_mod=seq // tm_p, p_dtype=BF16, hd=hd, dk=dk, with_vb=False,
                            kv_transposed=True)
        oa = _sb_attn(p, mb_p[l], u_p, batch=batch, seq=seq, tq=tq, n_heads=n_heads_a, hd=hd, pairs_per_body=4)
        ob, oc, s_fin = _mix_prompt(p.reshape(batch, seq, -1), gpow_p, w_spatial[l], bsp_pb[l], decay_p, xi_pb,
                                    zeta_pb, n_grp=n_grp_b, n_heads=n_heads_c)
        xp = _merge_mlp(xp, mod_p, l, g_mlp, p, oa, ob.reshape(batch * seq, GRP), oc.reshape(batch * seq, GRP),
                        wa_b, wb_b, wc_b, wo_b, w1_b, w2_b, tm=tm_p, tf=1024, rows_per_mod=seq // tm_p)
        kp_l.append(kp)
        vp_l.append(vp)
        sp_l.append(s_fin)

        ps, ks, vs, vbs = _inproj(xs, mod_s, l, g_mix, w_in_b, cos_s, sin_s, g_qa_t, g_ka_t, g_vb_r, gsum,
                                  tm=rows_s, rows_per_mod=1, p_dtype=F32, hd=hd, dk=dk, with_vb=True,
                                  kv_transposed=False)
        q_s = ps[:, J_QA * GRP:(J_QA + 1) * GRP].reshape(nb, n_tok, 1, w_a)
        qbd = jnp.where(qbd_mask, jnp.broadcast_to(q_s, (nb, n_tok, n_heads_a, w_a)).reshape(nb, -1, w_a), 0.0)
        pad = ((0, 0), (0, 0), (0, page - n_tok))
        knew_t = jnp.pad(jnp.transpose(ks.reshape(nb, n_tok, w_a), (0, 2, 1)), pad).astype(BF16)
        vnew_t = jnp.pad(jnp.transpose(vs.reshape(nb, n_tok, w_a), (0, 2, 1)), pad).astype(BF16)
        oas = _paged_attn(page_table, qbd.astype(BF16), bias_col[l], mb_new[l], knew_t, vnew_t, u_s, u_new,
                          cache_kt, cache_vt, layer=l, g_pages=g_pages, n_slots=4, n_tok=n_tok)
        obs, ocs, s_new = _mix_sample(ps, state_ret, l, gpow_s, mg_s[l], bg_s[l], dmat_s, xi_sb, zeta_sb,
                                      n_batch=nb, n_tok=n_tok, n_heads=n_heads_c)
        xs = _merge_mlp(xs, mod_s, l, g_mlp, ps, oas.reshape(rows_s, w_a), obs, ocs, wa_b, wb_b, wc_b, wo_b,
                        w1_b, w2_b, tm=rows_s, tf=1024, rows_per_mod=1)
        ks_l.append(ks)
        vs_l.append(vs)
        ss_l.append(s_new)
        gv_l.append(vbs)

    def kv(lst, b_, l_):
        return jnp.stack(lst).reshape(depth, b_, l_, n_heads_a, hd)

    def kv_t(lst):
        return jnp.transpose(jnp.stack(lst).reshape(depth, batch, n_heads_a, hd, seq), (0, 1, 4, 2, 3))

    return (xp.reshape(batch, seq, d), xs.reshape(nb, n_tok, d),
            kv_t(kp_l), kv_t(vp_l), kv(ks_l, nb, n_tok), kv(vs_l, nb, n_tok),
            jnp.stack(sp_l), jnp.stack(ss_l), jnp.stack(gv_l).reshape(depth, nb, n_tok, w_b))
```

```python
import functools

import jax
import jax.numpy as jnp
import numpy as np
from jax import lax
from jax.experimental import pallas as pl
from jax.experimental.pallas import tpu as pltpu

F32 = jnp.float32
BF16 = jnp.bfloat16

EPS = 1e-6
LOG2E = float(np.log2(np.e))
ROPE_BASE = 10000.0
RET_CHUNK = 128
LANES = 128
V7X_VMEM_BYTES = 64 * 1024 * 1024
VMEM_LIMIT = 56 * 1024 * 1024

GRP = 512
J_GA, J_GB, J_GCM = 0, 2, 4
J_QA, J_KA, J_VA, J_UB, J_VB, J_QC, J_KC, J_VC, J_GC = 6, 7, 8, 9, 10, 11, 12, 13, 14
N_GRP = 15


def _cparams(sem, vmem=VMEM_LIMIT):
    return pltpu.CompilerParams(dimension_semantics=sem, vmem_limit_bytes=vmem)


def _resident(block_shape, index_map):
    return pl.BlockSpec(block_shape, index_map, pipeline_mode=pl.Buffered(1))


def _sigmoid(x):
    return 1.0 / (1.0 + jnp.exp(-x))


def _center_norm(o):
    mu = jnp.mean(o, axis=-1, keepdims=True)
    d = o - mu
    var = jnp.mean(d * d, axis=-1, keepdims=True)
    return d * lax.rsqrt(var + EPS)


def _gelu_tanh(x):
    return 0.5 * x * (1.0 + jnp.tanh(np.sqrt(2.0 / np.pi).astype(np.float32) * (x + 0.044715 * (x * x * x))))


def _nt_dot(a, b):
    return lax.dot_general(a, b, (((1,), (1,)), ((), ())), preferred_element_type=F32)


def _adaln_kernel(c_ref, w_ref, b_ref, o_ref):
    c = c_ref[...]
    s = (c * _sigmoid(c)).astype(BF16)
    o_ref[...] = jnp.dot(s, w_ref[...].astype(BF16), preferred_element_type=F32) + b_ref[...]


def _adaln(c_all, w_ada, b_ada, tn=1536):
    depth, d, n = w_ada.shape
    m = c_all.shape[0]
    return pl.pallas_call(
        _adaln_kernel,
        grid=(depth, n // tn),
        in_specs=[
            pl.BlockSpec((m, d), lambda l, j: (0, 0)),
            pl.BlockSpec((None, d, tn), lambda l, j: (l, 0, j)),
            pl.BlockSpec((None, 1, tn), lambda l, j: (l, 0, j)),
        ],
        out_specs=pl.BlockSpec((None, m, tn), lambda l, j: (l, 0, j)),
        out_shape=jax.ShapeDtypeStruct((depth, m, n), F32),
        compiler_params=_cparams(("parallel", "parallel")),
        name="adaln",
    )(c_all, w_ada, b_ada.reshape(depth, 1, n))


def _inproj_kernel(x_ref, shift_ref, scale_ref, g_ref, w_ref, cos_ref, sin_ref, gq_ref, gk_ref, gvb_ref,
                   gsum_ref, *refs, hd_scale, kc_scale, with_vb, kv_transposed, n_aliased):
    p_ref, k_ref, v_ref = refs[n_aliased:n_aliased + 3]
    vb_ref = refs[n_aliased + 3] if with_vb else None
    h_scr = refs[-1]
    x = x_ref[...]
    ms = jnp.mean(x * x, axis=-1, keepdims=True)
    y = x * lax.rsqrt(ms + EPS) * g_ref[...]
    h_scr[...] = (y * (1.0 + scale_ref[...]) + shift_ref[...]).astype(BF16)

    def head_norm(a, g):
        ms = jnp.dot((a * a).astype(BF16), gsum_ref[...], preferred_element_type=F32)
        return a * lax.rsqrt(ms + EPS) * g

    def rotary(a):
        outs = []
        for h in range(GRP // LANES):
            xh = a[:, h * LANES:(h + 1) * LANES]
            outs.append(xh * cos_ref[...] + pltpu.roll(xh, LANES // 2, axis=1) * sin_ref[...])
        return jnp.concatenate(outs, axis=1)

    def activation(j, acc):
        if j < J_QA:
            return _sigmoid(acc)
        if j == J_QA:
            return head_norm(acc, gq_ref[...]) * hd_scale
        if j == J_KA:
            kn = head_norm(acc, gk_ref[...])
            k_ref[...] = kn.T if kv_transposed else kn
            return kn
        if j == J_VA:
            v_ref[...] = acc.T if kv_transposed else acc
            return acc
        if j == J_UB:
            return _gelu_tanh(acc)
        if j == J_VB:
            vb = _center_norm(_gelu_tanh(acc)) * gvb_ref[...]
            if with_vb:
                vb_ref[...] = vb
            return vb
        if j == J_QC:
            return rotary(acc)
        if j == J_KC:
            return rotary(acc) * kc_scale
        if j == J_VC:
            return acc
        assert j == J_GC
        return acc * _sigmoid(acc)

    for j in range(N_GRP):
        src = (j + N_GRP - J_QA) % N_GRP
        acc = jnp.dot(h_scr[...], w_ref[:, src * GRP:(src + 1) * GRP], preferred_element_type=F32)
        p_ref[:, j * GRP:(j + 1) * GRP] = activation(j, acc).astype(p_ref.dtype)


def _inproj(x2d, mod4, layer, g_norm, w_in_b, cos_t, sin_t, g_qa, g_ka, g_vb, gsum, *, tm, rows_per_mod,
            p_dtype, hd, dk, with_vb, kv_transposed, kv_all=()):
    m, d = x2d.shape
    n_in = w_in_b.shape[-1]
    assert n_in == N_GRP * GRP and m % tm == 0
    rm = mod4.shape[2]
    seq = cos_t.shape[0]
    n_pos_blocks = seq // tm
    mod_idx = (lambda chunk: (lambda i: (layer, i // rows_per_mod, 0, chunk)))
    kv_all = tuple(kv_all)
    kern = functools.partial(_inproj_kernel, hd_scale=float(hd) ** -0.5 * LOG2E, kc_scale=float(dk) ** -0.5,
                             with_vb=with_vb, kv_transposed=kv_transposed, n_aliased=len(kv_all))
    row_blk = lambda i: (i, 0)
    per_layer = lambda i: (layer, 0, 0)
    f32_specs = [pl.BlockSpec((tm, GRP), row_blk)] * (3 if with_vb else 2)
    f32_shapes = [jax.ShapeDtypeStruct((m, GRP), F32)] * (3 if with_vb else 2)
    if kv_transposed:
        depth = mod4.shape[0]
        f32_specs[:2] = [pl.BlockSpec((None, None, GRP, tm),
                                      lambda i: (layer, i // n_pos_blocks, 0, i % n_pos_blocks))] * 2
        f32_shapes[:2] = [jax.ShapeDtypeStruct((depth, m // seq, GRP, seq), F32)] * 2
    n_in_specs = 11
    return pl.pallas_call(
        kern,
        grid=(m // tm,),
        in_specs=[
            pl.BlockSpec((tm, d), row_blk),
            pl.BlockSpec((None, None, rm, d), mod_idx(0)),
            pl.BlockSpec((None, None, rm, d), mod_idx(1)),
            pl.BlockSpec((None, 1, d), per_layer),
            _resident((None, d, n_in), per_layer),
            pl.BlockSpec((tm, LANES), lambda i: (i % n_pos_blocks, 0)),
            pl.BlockSpec((tm, LANES), lambda i: (i % n_pos_blocks, 0)),
            pl.BlockSpec((None, 1, GRP), per_layer),
            pl.BlockSpec((None, 1, GRP), per_layer),
            pl.BlockSpec((None, 1, GRP), per_layer),
            _resident((GRP, GRP), lambda i: (0, 0)),
        ] + [pl.BlockSpec(memory_space=pl.ANY)] * len(kv_all),
        out_specs=[pl.BlockSpec((tm, n_in), row_blk)] + f32_specs,
        out_shape=[jax.ShapeDtypeStruct((m, n_in), p_dtype)] + f32_shapes,
        input_output_aliases={n_in_specs + a: 1 + a for a in range(len(kv_all))},
        scratch_shapes=[pltpu.VMEM((tm, d), BF16)],
        compiler_params=_cparams(("parallel",)),
        name="inproj",
    )(x2d, mod4, mod4, g_norm, w_in_b, cos_t, sin_t, g_qa, g_ka, g_vb, gsum, *kv_all)


def _neg_abs(x):
    return -jnp.abs(x)


def _sb_weights(z2, u, c):
    n = z2.shape[1]
    sp = jnp.maximum(z2, 0.0) + jnp.log2(1.0 + jnp.exp2(_neg_abs(z2)))
    sums = jnp.dot(sp.astype(BF16), u, preferred_element_type=F32)
    later = sums + jnp.concatenate([c] * (n // LANES), axis=1)
    a = jnp.exp2(z2 - sp - later).astype(BF16)
    return jnp.broadcast_to(sums[:, :1] + sp[:, :1], c.shape), a


def _sb_weights_chunked(z2, u, c):
    rows, n = z2.shape
    w = u.shape[0]
    n_chunks = n // w
    sp = jnp.maximum(z2, 0.0) + jnp.log2(1.0 + jnp.exp2(_neg_abs(z2)))
    stacked = jnp.concatenate([sp[:, j * w:(j + 1) * w] for j in range(n_chunks)], axis=0)
    sums = jnp.dot(stacked.astype(BF16), u, preferred_element_type=F32)
    later = [None] * n_chunks
    for j in reversed(range(n_chunks)):
        blk = sums[j * rows:(j + 1) * rows]
        later[j] = blk + jnp.concatenate([c] * (w // LANES), axis=1)
        c = c + jnp.broadcast_to(blk[:, :1] + sp[:, j * w:j * w + 1], c.shape)
    a = jnp.exp2(z2 - sp - jnp.concatenate(later, axis=1)).astype(BF16)
    return c, a


def _sb_attn_kernel(aug_ref, dmask_ref, q_ref, k_ref, v_ref, u_ref, o_ref, z_scr, acc_scr, c_scr, *, tq, n_heads, hd,
                    pairs_per_body):
    i = pl.program_id(1)
    heads_per_vreg = LANES // hd
    assert heads_per_vreg == 2
    n_pairs = n_heads // heads_per_vreg
    lane = lax.broadcasted_iota(jnp.int32, (1, LANES), 1)
    u = u_ref[...]
    in_head = [(lane >= sub * hd) & (lane < (sub + 1) * hd) for sub in range(heads_per_vreg)]
    bias_lanes = [(lane >= (1 - sub) * hd) & (lane < (1 - sub) * hd + 2) for sub in range(heads_per_vreg)]

    for grp in range(n_pairs // pairs_per_body):
        pairs = range(grp * pairs_per_body, (grp + 1) * pairs_per_body)
        qh = {}
        for pair in pairs:
            q128 = q_ref[:, pair * LANES:(pair + 1) * LANES]
            for sub in range(heads_per_vreg):
                h = pair * heads_per_vreg + sub
                qh[pair, sub] = jnp.where(in_head[sub], q128, jnp.broadcast_to(aug_ref[h], q128.shape))
                acc_scr[h] = jnp.zeros((tq, LANES), F32)
                c_scr[h] = jnp.zeros((tq, LANES), F32)

        def scores(kb, slot, diagonal):
            start = pl.multiple_of(jnp.maximum(kb, 0) * tq, tq)
            for pair in pairs:
                k128 = k_ref[pl.ds(start, tq), pair * LANES:(pair + 1) * LANES]
                for sub in range(heads_per_vreg):
                    h = pair * heads_per_vreg + sub
                    z = _nt_dot(qh[pair, sub], jnp.where(bias_lanes[sub], jnp.ones_like(k128), k128))
                    z_scr[h, slot] = z + dmask_ref[...] if diagonal else z

        def weigh(kb, slot):
            start = pl.multiple_of(kb * tq, tq)
            for pair in pairs:
                v128 = v_ref[pl.ds(start, tq), pair * LANES:(pair + 1) * LANES]
                for sub in range(heads_per_vreg):
                    h = pair * heads_per_vreg + sub
                    c = c_scr[h]
                    rs, a = _sb_weights(z_scr[h, slot], u, c)
                    acc_scr[h] += jnp.dot(a, v128, preferred_element_type=F32)
                    c_scr[h] = c + rs

        scores(i, 0, True)

        def body(m, carry):
            kb = i - 2 * m
            scores(kb - 1, 1, False)
            weigh(kb, 0)
            scores(kb - 2, 0, False)
            weigh(kb - 1, 1)
            return carry

        n_blocks = i + 1
        lax.fori_loop(0, n_blocks // 2, body, 0)

        @pl.when(n_blocks % 2 == 1)
        def _():
            weigh(0, 0)

        for pair in pairs:
            out128 = acc_scr[pair * heads_per_vreg]
            for sub in range(1, heads_per_vreg):
                out128 = jnp.where(in_head[sub], acc_scr[pair * heads_per_vreg + sub], out128)
            o_ref[:, pair * LANES:(pair + 1) * LANES] = out128.astype(o_ref.dtype)


def _sb_attn(p, aug, dmask, u, *, batch, seq, tq, n_heads, hd, pairs_per_body=1):
    nq = seq // tq
    width = n_heads * hd
    assert width == GRP
    kern = functools.partial(_sb_attn_kernel, tq=tq, n_heads=n_heads, hd=hd, pairs_per_body=pairs_per_body)
    return pl.pallas_call(
        kern,
        grid=(batch, nq),
        in_specs=[
            pl.BlockSpec((n_heads, 1, LANES), lambda b, i: (0, 0, 0)),
            pl.BlockSpec((tq, tq), lambda b, i: (0, 0)),
            pl.BlockSpec((tq, GRP), lambda b, i: (b * nq + i, J_QA)),
            pl.BlockSpec((seq, GRP), lambda b, i: (b, J_KA)),
            pl.BlockSpec((seq, GRP), lambda b, i: (b, J_VA)),
            pl.BlockSpec((tq, tq), lambda b, i: (0, 0)),
        ],
        out_specs=pl.BlockSpec((tq, GRP), lambda b, i: (b * nq + i, 0)),
        out_shape=jax.ShapeDtypeStruct((batch * seq, GRP), BF16),
        scratch_shapes=[pltpu.VMEM((n_heads, 2, tq, tq), F32), pltpu.VMEM((n_heads, tq, LANES), F32),
                        pltpu.VMEM((n_heads, tq, LANES), F32)],
        compiler_params=_cparams(("parallel", "arbitrary")),
        name="sb_attn",
    )(aug, dmask, p, p, p, u)


def _paged_attn_kernel(pt_ref, qbd_ref, bias_ref, mbnew_ref, knew_ref, vnew_ref, u_ref, unew_ref, ck_hbm, cv_hbm,
                       o_ref, kbuf, vbuf, sem, *, layer, n_batch, n_groups, g_pages, page, n_heads, hd, n_tok):
    b = pl.program_id(0)
    n_pages = n_groups * g_pages
    width = n_heads * hd
    n_slots = kbuf.shape[0]
    total = n_batch * n_groups

    def copies(g, slot):
        bb = g // n_groups
        gi = g % n_groups
        out = []
        for pi in range(g_pages):
            pg = pt_ref[bb, n_pages - (gi + 1) * g_pages + pi]
            out.append((pltpu.make_async_copy(ck_hbm.at[layer, pg], kbuf.at[slot, pi], sem.at[0, slot]),
                        pltpu.make_async_copy(cv_hbm.at[layer, pg], vbuf.at[slot, pi], sem.at[1, slot])))
        return out

    def start(g):
        for ck, cv in copies(g, g % n_slots):
            ck.start(priority=0)
            cv.start(priority=1)

    @pl.when(b == 0)
    def _():
        for g in range(min(n_slots - 1, total)):
            start(g)

    qbd = qbd_ref[...]
    bias = bias_ref[...]
    rows = qbd.shape[0]

    rs, a = _sb_weights(jnp.dot(qbd, knew_ref[...], preferred_element_type=F32) + mbnew_ref[...], unew_ref[...],
                        jnp.zeros((rows, LANES), F32))
    c = rs
    acc = _nt_dot(a, vnew_ref[...])

    def body(gi, carry):
        g = b * n_groups + gi
        slot = g % n_slots

        @pl.when(g + n_slots - 1 < total)
        def _():
            start(g + n_slots - 1)

        for ck, cv in copies(g, slot):
            ck.wait()
            cv.wait()
        c, acc = carry
        z2 = jnp.concatenate(
            [jnp.dot(qbd, kbuf[slot, pi].reshape(width, page).astype(BF16), preferred_element_type=F32)
             for pi in range(g_pages)], axis=1) + bias
        c, a = _sb_weights_chunked(z2, u_ref[...], c)
        for pi in range(g_pages):
            acc = acc + _nt_dot(a[:, pi * page:(pi + 1) * page], vbuf[slot, pi].reshape(width, page).astype(BF16))
        return c, acc

    c, acc = lax.fori_loop(0, n_groups, body, (c, acc))

    rr = lax.broadcasted_iota(jnp.int32, acc.shape, 0) % n_heads
    cc = lax.broadcasted_iota(jnp.int32, acc.shape, 1) // hd
    own = jnp.where(rr == cc, acc, 0.0)
    o_ref[...] = jnp.concatenate(
        [jnp.sum(own[t * n_heads:(t + 1) * n_heads], axis=0, keepdims=True) for t in range(n_tok)], axis=0)


def _paged_attn(page_table, qbd, bias_col, mb_new, knew_t, vnew_t, u, u_new, cache_kt, cache_vt, *, layer,
                g_pages, n_slots, n_tok):
    nb, n_pages = page_table.shape
    _, _, n_heads, hd, page = cache_kt.shape
    width = n_heads * hd
    rows = qbd.shape[1]
    assert n_pages % g_pages == 0 and rows == n_tok * n_heads and (g_pages * page) % u.shape[0] == 0
    kern = functools.partial(_paged_attn_kernel, layer=layer, n_batch=nb, n_groups=n_pages // g_pages, g_pages=g_pages,
                             page=page, n_heads=n_heads, hd=hd, n_tok=n_tok)
    const2 = lambda b, pt: (0, 0)
    per_b = lambda b, pt: (b, 0, 0)
    grid_spec = pltpu.PrefetchScalarGridSpec(
        num_scalar_prefetch=1,
        grid=(nb,),
        in_specs=[
            pl.BlockSpec((None, rows, width), per_b),
            pl.BlockSpec((rows, 1), const2),
            pl.BlockSpec((rows, page), const2),
            pl.BlockSpec((None, width, page), per_b),
            pl.BlockSpec((None, width, page), per_b),
            pl.BlockSpec(u.shape, const2),
            pl.BlockSpec((page, page), const2),
            pl.BlockSpec(memory_space=pl.ANY),
            pl.BlockSpec(memory_space=pl.ANY),
        ],
        out_specs=pl.BlockSpec((None, n_tok, width), per_b),
        scratch_shapes=[
            pltpu.VMEM((n_slots, g_pages, n_heads, hd, page), F32),
            pltpu.VMEM((n_slots, g_pages, n_heads, hd, page), F32),
            pltpu.SemaphoreType.DMA((2, n_slots)),
        ],
    )
    return pl.pallas_call(
        kern,
        grid_spec=grid_spec,
        out_shape=jax.ShapeDtypeStruct((nb, n_tok, width), F32),
        compiler_params=_cparams(("arbitrary",)),
        name="paged_attn",
    )(page_table, qbd, bias_col, mb_new, knew_t, vnew_t, u, u_new, cache_kt, cache_vt)


def _mix_prompt_kernel(gpow_ref, ub_ref, vb_ref, qc_ref, kc_ref, vc_ref, gc_ref, wsp_ref, bsp_ref, decay_ref,
                       xi_ref, zeta_ref, ob_ref, oc_ref, sfin_ref, state, *, batch, n_grp, n_heads, t):
    ci = pl.program_id(0)

    @pl.when(ci == 0)
    def _():
        state[...] = jnp.zeros_like(state)

    r = lax.broadcasted_iota(jnp.int32, (t, t), 0)
    s = lax.broadcasted_iota(jnp.int32, (t, t), 1)
    tril = s <= r
    for b, j in [(b, j) for b in range(batch) for j in range(ub_ref.shape[1] // t)]:
        rows = slice(j * t, (j + 1) * t)
        for g in range(n_grp):
            lanes = slice(g * LANES, (g + 1) * LANES)
            w = jnp.where(tril, wsp_ref[g], 0.0).astype(BF16)
            mixed = jnp.dot(w, vb_ref[b, rows, lanes], preferred_element_type=F32) + bsp_ref[g]
            ob_ref[b, rows, lanes] = (ub_ref[b, rows, lanes].astype(F32) * mixed).astype(ob_ref.dtype)
        for h in range(n_heads):
            lanes = slice(h * LANES, (h + 1) * LANES)
            q = qc_ref[b, rows, lanes]
            k = kc_ref[b, rows, lanes]
            v = vc_ref[b, rows, lanes]
            st = state[b, h]
            scores = _nt_dot(q, k) * decay_ref[h]
            inner = jnp.dot(scores.astype(BF16), v, preferred_element_type=F32)
            cross = jnp.dot(q, st.astype(BF16), preferred_element_type=F32) * xi_ref[h]
            kz = (k.astype(F32) * zeta_ref[h]).astype(BF16)
            upd = lax.dot_general(kz, v, (((0,), (0,)), ((), ())), preferred_element_type=F32)
            state[b, h] = gpow_ref[h] * st + upd
            on = _center_norm(inner + cross)
            oc_ref[b, rows, lanes] = (on * gc_ref[b, rows, lanes].astype(F32)).astype(oc_ref.dtype)

    @pl.when(ci == pl.num_programs(0) - 1)
    def _():
        sfin_ref[...] = state[...]


def _mix_prompt(p3, gpow, wsp_l, bsp_b, decay, xi_b, zeta_b, *, n_grp, n_heads, chunks_per_step):
    batch, seq, _ = p3.shape
    t = RET_CHUNK
    ts = chunks_per_step * t
    assert seq % ts == 0
    kern = functools.partial(_mix_prompt_kernel, batch=batch, n_grp=n_grp, n_heads=n_heads, t=t)
    col = lambda jj: (lambda c: (0, c, jj))
    const3 = lambda c: (0, 0, 0)
    return pl.pallas_call(
        kern,
        grid=(seq // ts,),
        in_specs=[
            pl.BlockSpec(memory_space=pltpu.SMEM),
            pl.BlockSpec((batch, ts, GRP), col(J_UB)),
            pl.BlockSpec((batch, ts, GRP), col(J_VB)),
            pl.BlockSpec((batch, ts, GRP), col(J_QC)),
            pl.BlockSpec((batch, ts, GRP), col(J_KC)),
            pl.BlockSpec((batch, ts, GRP), col(J_VC)),
            pl.BlockSpec((batch, ts, GRP), col(J_GC)),
            pl.BlockSpec((n_grp, t, t), const3),
            pl.BlockSpec((n_grp, t, LANES), const3),
            pl.BlockSpec((n_heads, t, t), const3),
            pl.BlockSpec((n_heads, t, LANES), const3),
            pl.BlockSpec((n_heads, t, LANES), const3),
        ],
        out_specs=[
            pl.BlockSpec((batch, ts, GRP), lambda c: (0, c, 0)),
            pl.BlockSpec((batch, ts, GRP), lambda c: (0, c, 0)),
            pl.BlockSpec((batch, n_heads, LANES, LANES), lambda c: (0, 0, 0, 0)),
        ],
        out_shape=[
            jax.ShapeDtypeStruct((batch, seq, GRP), BF16),
            jax.ShapeDtypeStruct((batch, seq, GRP), BF16),
            jax.ShapeDtypeStruct((batch, n_heads, LANES, LANES), F32),
        ],
        scratch_shapes=[pltpu.VMEM((batch, n_heads, LANES, LANES), F32)],
        compiler_params=_cparams(("arbitrary",)),
        name="mix_prompt",
    )(gpow, p3, p3, p3, p3, p3, p3, wsp_l, bsp_b, decay, xi_b, zeta_b)


def _mix_sample_kernel(gpow_ref, ub_ref, vb_ref, qc_ref, kc_ref, vc_ref, gc_ref, mg_ref, bg_ref, dmat_ref,
                       xi_ref, zeta_ref, st_ref, ob_ref, oc_ref, snew_ref, *, n_batch, n_tok):
    h = pl.program_id(0)
    rows = n_batch * n_tok
    mixed = jnp.dot(mg_ref[...].astype(BF16), vb_ref[...].astype(BF16), preferred_element_type=F32) + bg_ref[...]
    ob_ref[...] = ub_ref[...] * mixed

    q = qc_ref[...].astype(BF16)
    k = kc_ref[...]
    v = vc_ref[...].astype(BF16)
    scores = _nt_dot(q, k.astype(BF16)) * dmat_ref[...]
    inner = jnp.dot(scores.astype(BF16), v, preferred_element_type=F32)
    kzt = (k * zeta_ref[...]).T
    gp = gpow_ref[h]
    row_b = lax.broadcasted_iota(jnp.int32, (rows, LANES), 0) // n_tok
    col_b = lax.broadcasted_iota(jnp.int32, (LANES, rows), 1) // n_tok

    def body(b, cross):
        st = st_ref[b]
        cr = jnp.dot(q, st.astype(BF16), preferred_element_type=F32)
        cross = jnp.where(row_b == b, cr, cross)
        upd = jnp.dot(jnp.where(col_b == b, kzt, 0.0).astype(BF16), v, preferred_element_type=F32)
        snew_ref[b] = gp * st + upd
        return cross

    cross = lax.fori_loop(0, n_batch, body, jnp.zeros((rows, LANES), F32))
    on = _center_norm(inner + cross * xi_ref[...])
    oc_ref[...] = on * gc_ref[...]


def _mix_sample(ps, state_ret, layer, gpow4, mg, bg, dmat, xi_s, zeta_s, *, n_batch, n_tok, n_heads):
    rows = n_batch * n_tok
    kern = functools.partial(_mix_sample_kernel, n_batch=n_batch, n_tok=n_tok)
    col = lambda jj: (lambda h: (0, jj * (GRP // LANES) + h))
    per_h = lambda h: (h, 0, 0)
    return pl.pallas_call(
        kern,
        grid=(n_heads,),
        in_specs=[
            pl.BlockSpec(memory_space=pltpu.SMEM),
            pl.BlockSpec((rows, LANES), col(J_UB)),
            pl.BlockSpec((rows, LANES), col(J_VB)),
            pl.BlockSpec((rows, LANES), col(J_QC)),
            pl.BlockSpec((rows, LANES), col(J_KC)),
            pl.BlockSpec((rows, LANES), col(J_VC)),
            pl.BlockSpec((rows, LANES), col(J_GC)),
            pl.BlockSpec((None, rows, rows), per_h),
            pl.BlockSpec((None, rows, LANES), per_h),
            pl.BlockSpec((None, rows, rows), per_h),
            pl.BlockSpec((None, rows, LANES), per_h),
            pl.BlockSpec((None, rows, LANES), per_h),
            pl.BlockSpec((None, n_batch, None, LANES, LANES), lambda h: (layer, 0, h, 0, 0)),
        ],
        out_specs=[
            pl.BlockSpec((rows, LANES), lambda h: (0, h)),
            pl.BlockSpec((rows, LANES), lambda h: (0, h)),
            pl.BlockSpec((n_batch, None, LANES, LANES), lambda h: (0, h, 0, 0)),
        ],
        out_shape=[
            jax.ShapeDtypeStruct((rows, GRP), F32),
            jax.ShapeDtypeStruct((rows, GRP), F32),
            jax.ShapeDtypeStruct((n_batch, n_heads, LANES, LANES), F32),
        ],
        compiler_params=_cparams(("parallel",)),
        name="mix_sample",
    )(gpow4, ps, ps, ps, ps, ps, ps, mg, bg, dmat, xi_s, zeta_s, state_ret)


def _merge_mlp_kernel(x_ref, gate1_ref, shift_ref, scale_ref, gate2_ref, g_ref, oa_ref, ob_ref, oc_ref, ga_ref,
                      gb_ref, gcm_ref, wa_ref, wb_ref, wc_ref, wo_ref, w1_ref, w2_ref, o_ref, h_scr, *, tf):
    def br(o_r, w_r, g_r):
        return g_r[...].astype(F32) * jnp.dot(o_r[...].astype(BF16), w_r[...], preferred_element_type=F32)

    merged = br(oa_ref, wa_ref, ga_ref) + br(ob_ref, wb_ref, gb_ref) + br(oc_ref, wc_ref, gcm_ref)
    x1 = x_ref[...] + gate1_ref[...] * jnp.dot(merged.astype(BF16), wo_ref[...], preferred_element_type=F32)
    o_ref[...] = x1
    ms = jnp.mean(x1 * x1, axis=-1, keepdims=True)
    y = x1 * lax.rsqrt(ms + EPS) * g_ref[...]
    h_scr[...] = (y * (1.0 + scale_ref[...]) + shift_ref[...]).astype(BF16)
    acc = None
    for j in range(w1_ref.shape[1] // tf):
        a = jnp.maximum(jnp.dot(h_scr[...], w1_ref[:, j * tf:(j + 1) * tf], preferred_element_type=F32), 0.0)
        t = jnp.dot((a * a).astype(BF16), w2_ref[j * tf:(j + 1) * tf, :], preferred_element_type=F32)
        acc = t if acc is None else acc + t
    o_ref[...] = o_ref[...] + gate2_ref[...] * acc


def _merge_mlp(x2d, mod4, layer, g_norm, p, oa, ob, oc, wa, wb, wc, wo, w1, w2, *, tm, tf, rows_per_mod):
    m, d = x2d.shape
    rm = mod4.shape[2]
    row = lambda i: (i, 0)
    mod_idx = (lambda chunk: (lambda i: (layer, i // rows_per_mod, 0, chunk)))
    wspec = lambda w: _resident((None,) + w.shape[1:], lambda i: (layer, 0, 0))
    return pl.pallas_call(
        functools.partial(_merge_mlp_kernel, tf=tf),
        grid=(m // tm,),
        in_specs=[
            pl.BlockSpec((tm, d), row),
            pl.BlockSpec((None, None, rm, d), mod_idx(2)),
            pl.BlockSpec((None, None, rm, d), mod_idx(3)),
            pl.BlockSpec((None, None, rm, d), mod_idx(4)),
            pl.BlockSpec((None, None, rm, d), mod_idx(5)),
            pl.BlockSpec((None, 1, d), lambda i: (layer, 0, 0)),
            pl.BlockSpec((tm, GRP), row),
            pl.BlockSpec((tm, GRP), row),
            pl.BlockSpec((tm, GRP), row),
            pl.BlockSpec((tm, d), lambda i: (i, J_GA // 2)),
            pl.BlockSpec((tm, d), lambda i: (i, J_GB // 2)),
            pl.BlockSpec((tm, d), lambda i: (i, J_GCM // 2)),
            wspec(wa), wspec(wb), wspec(wc), wspec(wo), wspec(w1), wspec(w2),
        ],
        out_specs=pl.BlockSpec((tm, d), row),
        out_shape=jax.ShapeDtypeStruct((m, d), F32),
        scratch_shapes=[pltpu.VMEM((tm, d), BF16)],
        compiler_params=_cparams(("parallel",)),
        name="merge_mlp",
    )(x2d, mod4, mod4, mod4, mod4, g_norm, oa, ob, oc, p, p, p, wa, wb, wc, wo, w1, w2)


def _rope_tables(pos, dk):
    half = dk // 2
    freqs = ROPE_BASE ** (-jnp.arange(half, dtype=F32) / half)
    ang = pos.astype(F32)[:, None] * freqs[None, :]
    cos, sin = jnp.cos(ang), jnp.sin(ang)
    return jnp.concatenate([cos, cos], axis=1), jnp.concatenate([-sin, sin], axis=1)


def kernel(x_prompt, x_sample, cache_k, cache_v, state_ret, page_table, c_prompt, c_sample, w_ada, b_ada, g_norm_mix, g_norm_mlp, w_in, g_qa, g_ka, sb_bias, g_vb, w_spatial, b_spatial, w_proj_a, w_proj_b, w_proj_c, w_out, w_mlp1, w_mlp2):
    batch, seq, d = x_prompt.shape
    nb, n_tok, _ = x_sample.shape
    depth, n_pool, page, n_heads_a, hd = cache_k.shape
    n_pages = page_table.shape[1]
    past = n_pages * page
    n_grp_b, chunk_b = w_spatial.shape[1], w_spatial.shape[2]
    w_b = g_vb.shape[1]
    n_heads_c, dk, dv = state_ret.shape[2:]
    w_a = n_heads_a * hd
    w_c = n_heads_c * dv
    rows_s = nb * n_tok
    assert w_a == GRP and w_b == GRP and w_c == GRP and n_heads_c * dk == GRP and d == 2 * GRP
    assert dk == LANES and dv == LANES and w_b // n_grp_b == LANES and chunk_b == RET_CHUNK and n_grp_b == n_heads_c
    assert seq % RET_CHUNK == 0 and n_tok <= chunk_b

    w_in_b = w_in.astype(BF16)
    wa_b, wb_b, wc_b, wo_b = (w.astype(BF16) for w in (w_proj_a, w_proj_b, w_proj_c, w_out))
    w1_b, w2_b = w_mlp1.astype(BF16), w_mlp2.astype(BF16)
    g_mix = g_norm_mix.reshape(depth, 1, d)
    g_mlp = g_norm_mlp.reshape(depth, 1, d)
    g_qa_t = jnp.tile(g_qa, (1, n_heads_a)).reshape(depth, 1, GRP)
    g_ka_t = jnp.tile(g_ka, (1, n_heads_a)).reshape(depth, 1, GRP)
    g_vb_r = g_vb.reshape(depth, 1, GRP)
    gi = np.arange(GRP) // hd
    gsum = jnp.asarray((gi[:, None] == gi[None, :]).astype(np.float32) / hd, dtype=BF16)

    log_gamma = jnp.log1p(-jnp.exp2(-5.0 - jnp.arange(n_heads_c, dtype=F32)))

    def ret_tables(length):
        i = jnp.arange(length, dtype=F32)
        diff = i[:, None] - i[None, :]
        decay = jnp.where(diff >= 0, jnp.exp(log_gamma[:, None, None] * jnp.maximum(diff, 0.0)), 0.0)
        xi = jnp.exp(log_gamma[:, None] * (i[None, :] + 1.0))
        zeta = jnp.exp(log_gamma[:, None] * (length - 1.0 - i[None, :]))
        return decay, xi, zeta, jnp.exp(log_gamma * length)

    decay_p, xi_p, zeta_p, gpow_p = ret_tables(RET_CHUNK)
    xi_pb = jnp.broadcast_to(xi_p[:, :, None], (n_heads_c, RET_CHUNK, LANES))
    zeta_pb = jnp.broadcast_to(zeta_p[:, :, None], (n_heads_c, RET_CHUNK, LANES))
    decay_s, xi_s, zeta_s, gpow_s = ret_tables(n_tok)
    eye_b = jnp.eye(nb, dtype=F32)
    dmat_s = jnp.stack([jnp.kron(eye_b, decay_s[h]) for h in range(n_heads_c)])
    xi_sb = jnp.broadcast_to(jnp.tile(xi_s, (1, nb))[:, :, None], (n_heads_c, rows_s, LANES))
    zeta_sb = jnp.broadcast_to(jnp.tile(zeta_s, (1, nb))[:, :, None], (n_heads_c, rows_s, LANES))

    bsp_pb = jnp.broadcast_to(b_spatial[:, :, :, None], (depth, n_grp_b, chunk_b, LANES))
    w_sp_s = jnp.tril(w_spatial[:, :, :n_tok, :n_tok])
    mg_s = jnp.einsum("ab,lgts->lgatbs", eye_b, w_sp_s).reshape(depth, n_grp_b, rows_s, rows_s)
    bg_s = jnp.broadcast_to(jnp.tile(b_spatial[:, :, :n_tok], (1, 1, nb))[..., None],
                            (depth, n_grp_b, rows_s, LANES))

    cos_p, sin_p = _rope_tables(jnp.arange(seq), dk)
    cos_s, sin_s = _rope_tables(past + (jnp.arange(rows_s) % n_tok), dk)

    def later_sum_matrix(n):
        tri = np.arange(n)
        return jnp.asarray((tri[:, None] > tri[None, :]).astype(np.float32), dtype=BF16)

    tq = 256
    g_pages = 8
    assert n_tok <= page
    u_p = later_sum_matrix(tq)
    u_s = later_sum_matrix(512)
    u_new = later_sum_matrix(page)

    m_all = batch + nb
    m_pad = -(-m_all // 8) * 8
    c_all = jnp.concatenate([c_prompt, c_sample, jnp.zeros((m_pad - m_all, d), F32)], axis=0)
    mod = _adaln(c_all, w_ada, b_ada)
    mod_p = mod[:, :batch].reshape(depth, batch, 1, 6 * d)
    mod_s = jnp.repeat(mod[:, batch:m_all], n_tok, axis=1).reshape(depth, 1, rows_s, 6 * d)

    sb_bias2 = sb_bias * LOG2E
    bias_hi = sb_bias2.astype(BF16)
    bias_lo = (sb_bias2 - bias_hi.astype(F32)).astype(BF16)
    first_bias_lane = (1 - np.arange(n_heads_a) % 2) * hd
    lane_idx = np.arange(LANES)[None, :]
    aug_p = (jnp.where(jnp.asarray(lane_idx == first_bias_lane[:, None])[None], bias_hi[:, :, None], 0)
             + jnp.where(jnp.asarray(lane_idx == first_bias_lane[:, None] + 1)[None], bias_lo[:, :, None], 0)
             ).astype(BF16).reshape(depth, n_heads_a, 1, LANES)
    dmask_p = jnp.where(jnp.asarray(np.arange(tq)[None, :] < np.arange(tq)[:, None]), 0.0, -jnp.inf).astype(F32)
    q_head = np.arange(n_tok * n_heads_a) % n_heads_a
    q_tok = np.arange(n_tok * n_heads_a) // n_heads_a
    row_bias = sb_bias2[:, q_head]
    bias_col = row_bias[:, :, None]
    visible_new = jnp.asarray(np.arange(page)[None, :] < q_tok[:, None])
    mb_new = jnp.where(visible_new[None], bias_col, -jnp.inf)
    qbd_mask = jnp.asarray(q_head[:, None] == (np.arange(w_a) // hd)[None, :])
    cache_kt = jnp.transpose(cache_k, (0, 1, 3, 4, 2))
    cache_vt = jnp.transpose(cache_v, (0, 1, 3, 4, 2))

    tm_p = 512
    xp = x_prompt.reshape(batch * seq, d)
    xs = x_sample.reshape(rows_s, d)
    ks_l, vs_l, sp_l, ss_l, gv_l = [], [], [], [], []
    kp_all = jnp.zeros((depth, batch, w_a, seq), F32)
    vp_all = jnp.zeros((depth, batch, w_a, seq), F32)
    for l in range(depth):
        p, kp_all, vp_all = _inproj(xp, mod_p, l, g_mix, w_in_b, cos_p, sin_p, g_qa_t, g_ka_t, g_vb_r, gsum,
                                    tm=tm_p, rows_per_mod=seq // tm_p, p_dtype=BF16, hd=hd, dk=dk, with_vb=False,
                                    kv_transposed=True, kv_all=(kp_all, vp_all))
        oa = _sb_attn(p, aug_p[l], dmask_p, u_p, batch=batch, seq=seq, tq=tq, n_heads=n_heads_a, hd=hd, pairs_per_body=4)
        ob, oc, s_fin = _mix_prompt(p.reshape(batch, seq, -1), gpow_p, w_spatial[l], bsp_pb[l], decay_p, xi_pb,
                                    zeta_pb, n_grp=n_grp_b, n_heads=n_heads_c, chunks_per_step=4)
        xp = _merge_mlp(xp, mod_p, l, g_mlp, p, oa, ob.reshape(batch * seq, GRP), oc.reshape(batch * seq, GRP),
                        wa_b, wb_b, wc_b, wo_b, w1_b, w2_b, tm=tm_p, tf=1024, rows_per_mod=seq // tm_p)
        sp_l.append(s_fin)

        ps, ks, vs, vbs = _inproj(xs, mod_s, l, g_mix, w_in_b, cos_s, sin_s, g_qa_t, g_ka_t, g_vb_r, gsum,
                                  tm=rows_s, rows_per_mod=1, p_dtype=F32, hd=hd, dk=dk, with_vb=True,
                                  kv_transposed=False)
        q_s = ps[:, J_QA * GRP:(J_QA + 1) * GRP].reshape(nb, n_tok, 1, w_a)
        qbd = jnp.where(qbd_mask, jnp.broadcast_to(q_s, (nb, n_tok, n_heads_a, w_a)).reshape(nb, -1, w_a), 0.0)
        pad = ((0, 0), (0, 0), (0, page - n_tok))
        knew_t = jnp.pad(jnp.transpose(ks.reshape(nb, n_tok, w_a), (0, 2, 1)), pad).astype(BF16)
        vnew_t = jnp.pad(jnp.transpose(vs.reshape(nb, n_tok, w_a), (0, 2, 1)), pad).astype(BF16)
        oas = _paged_attn(page_table, qbd.astype(BF16), bias_col[l], mb_new[l], knew_t, vnew_t, u_s, u_new,
                          cache_kt, cache_vt, layer=l, g_pages=g_pages, n_slots=4, n_tok=n_tok)
        obs, ocs, s_new = _mix_sample(ps, state_ret, l, gpow_s, mg_s[l], bg_s[l], dmat_s, xi_sb, zeta_sb,
                                      n_batch=nb, n_tok=n_tok, n_heads=n_heads_c)
        xs = _merge_mlp(xs, mod_s, l, g_mlp, ps, oas.reshape(rows_s, w_a), obs, ocs, wa_b, wb_b, wc_b, wo_b,
                        w1_b, w2_b, tm=rows_s, tf=1024, rows_per_mod=1)
        ks_l.append(ks)
        vs_l.append(vs)
        ss_l.append(s_new)
        gv_l.append(vbs)

    def kv(lst, b_, l_):
        return jnp.stack(lst).reshape(depth, b_, l_, n_heads_a, hd)

    def kv_t(buf):
        return jnp.transpose(buf.reshape(depth, batch, n_heads_a, hd, seq), (0, 1, 4, 2, 3))

    return (xp.reshape(batch, seq, d), xs.reshape(nb, n_tok, d),
            kv_t(kp_all), kv_t(vp_all), kv(ks_l, nb, n_tok), kv(vs_l, nb, n_tok),
            jnp.stack(sp_l), jnp.stack(ss_l), jnp.stack(gv_l).reshape(depth, nb, n_tok, w_b))
```

```python
import functools

import jax
import jax.numpy as jnp
import numpy as np
from jax import lax
from jax.experimental import pallas as pl
from jax.experimental.pallas import tpu as pltpu

F32 = jnp.float32
BF16 = jnp.bfloat16

EPS = 1e-6
LOG2E = float(np.log2(np.e))
ROPE_BASE = 10000.0
RET_CHUNK = 128
LANES = 128
V7X_VMEM_BYTES = 64 * 1024 * 1024
VMEM_LIMIT = 56 * 1024 * 1024

GRP = 512
J_GA, J_GB, J_GCM = 0, 2, 4
J_QA, J_KA, J_VA, J_UB, J_VB, J_QC, J_KC, J_VC, J_GC = 6, 7, 8, 9, 10, 11, 12, 13, 14
N_GRP = 15


def _cparams(sem, vmem=VMEM_LIMIT):
    return pltpu.CompilerParams(dimension_semantics=sem, vmem_limit_bytes=vmem)


def _resident(block_shape, index_map):
    return pl.BlockSpec(block_shape, index_map, pipeline_mode=pl.Buffered(1))


def _sigmoid(x):
    return 1.0 / (1.0 + jnp.exp(-x))


def _center_norm(o):
    mu = jnp.mean(o, axis=-1, keepdims=True)
    d = o - mu
    var = jnp.mean(d * d, axis=-1, keepdims=True)
    return d * lax.rsqrt(var + EPS)


def _gelu_tanh(x):
    return 0.5 * x * (1.0 + jnp.tanh(np.sqrt(2.0 / np.pi).astype(np.float32) * (x + 0.044715 * (x * x * x))))


def _nt_dot(a, b):
    return lax.dot_general(a, b, (((1,), (1,)), ((), ())), preferred_element_type=F32)


def _adaln_kernel(c_ref, w_ref, b_ref, o_ref):
    c = c_ref[...]
    s = (c * _sigmoid(c)).astype(BF16)
    o_ref[...] = jnp.dot(s, w_ref[...].astype(BF16), preferred_element_type=F32) + b_ref[...]


def _adaln(c_all, w_ada, b_ada, tn=1536):
    depth, d, n = w_ada.shape
    m = c_all.shape[0]
    return pl.pallas_call(
        _adaln_kernel,
        grid=(depth, n // tn),
        in_specs=[
            pl.BlockSpec((m, d), lambda l, j: (0, 0)),
            pl.BlockSpec((None, d, tn), lambda l, j: (l, 0, j)),
            pl.BlockSpec((None, 1, tn), lambda l, j: (l, 0, j)),
        ],
        out_specs=pl.BlockSpec((None, m, tn), lambda l, j: (l, 0, j)),
        out_shape=jax.ShapeDtypeStruct((depth, m, n), F32),
        compiler_params=_cparams(("parallel", "parallel")),
        name="adaln",
    )(c_all, w_ada, b_ada.reshape(depth, 1, n))


def _inproj_kernel(x_ref, shift_ref, scale_ref, g_ref, w_ref, cos_ref, sin_ref, gq_ref, gk_ref, gvb_ref,
                   gsum_ref, *refs, hd_scale, kc_scale, with_vb, kv_transposed, n_aliased):
    p_ref, k_ref, v_ref = refs[n_aliased:n_aliased + 3]
    vb_ref = refs[n_aliased + 3] if with_vb else None
    h_scr = refs[-1]
    x = x_ref[...]
    ms = jnp.mean(x * x, axis=-1, keepdims=True)
    y = x * lax.rsqrt(ms + EPS) * g_ref[...]
    h_scr[...] = (y * (1.0 + scale_ref[...]) + shift_ref[...]).astype(BF16)

    def head_norm(a, g):
        ms = jnp.dot((a * a).astype(BF16), gsum_ref[...], preferred_element_type=F32)
        return a * lax.rsqrt(ms + EPS) * g

    def rotary(a):
        outs = []
        for h in range(GRP // LANES):
            xh = a[:, h * LANES:(h + 1) * LANES]
            outs.append(xh * cos_ref[...] + pltpu.roll(xh, LANES // 2, axis=1) * sin_ref[...])
        return jnp.concatenate(outs, axis=1)

    def activation(j, acc):
        if j < J_QA:
            return _sigmoid(acc)
        if j == J_QA:
            return head_norm(acc, gq_ref[...]) * hd_scale
        if j == J_KA:
            kn = head_norm(acc, gk_ref[...])
            k_ref[...] = kn.T if kv_transposed else kn
            return kn
        if j == J_VA:
            v_ref[...] = acc.T if kv_transposed else acc
            return acc
        if j == J_UB:
            return _gelu_tanh(acc)
        if j == J_VB:
            vb = _center_norm(_gelu_tanh(acc)) * gvb_ref[...]
            if with_vb:
                vb_ref[...] = vb
            return vb
        if j == J_QC:
            return rotary(acc)
        if j == J_KC:
            return rotary(acc) * kc_scale
        if j == J_VC:
            return acc
        assert j == J_GC
        return acc * _sigmoid(acc)

    for j in range(N_GRP):
        src = (j + N_GRP - J_QA) % N_GRP
        acc = jnp.dot(h_scr[...], w_ref[:, src * GRP:(src + 1) * GRP], preferred_element_type=F32)
        p_ref[:, j * GRP:(j + 1) * GRP] = activation(j, acc).astype(p_ref.dtype)


def _inproj(x2d, mod4, layer, g_norm, w_in_b, cos_t, sin_t, g_qa, g_ka, g_vb, gsum, *, tm, rows_per_mod,
            p_dtype, hd, dk, with_vb, kv_transposed, kv_all=()):
    m, d = x2d.shape
    n_in = w_in_b.shape[-1]
    assert n_in == N_GRP * GRP and m % tm == 0
    rm = mod4.shape[2]
    seq = cos_t.shape[0]
    n_pos_blocks = seq // tm
    mod_idx = (lambda chunk: (lambda i: (layer, i // rows_per_mod, 0, chunk)))
    kv_all = tuple(kv_all)
    kern = functools.partial(_inproj_kernel, hd_scale=float(hd) ** -0.5 * LOG2E, kc_scale=float(dk) ** -0.5,
                             with_vb=with_vb, kv_transposed=kv_transposed, n_aliased=len(kv_all))
    row_blk = lambda i: (i, 0)
    per_layer = lambda i: (layer, 0, 0)
    f32_specs = [pl.BlockSpec((tm, GRP), row_blk)] * (3 if with_vb else 2)
    f32_shapes = [jax.ShapeDtypeStruct((m, GRP), F32)] * (3 if with_vb else 2)
    if kv_transposed:
        depth = mod4.shape[0]
        f32_specs[:2] = [pl.BlockSpec((None, None, GRP, tm),
                                      lambda i: (layer, i // n_pos_blocks, 0, i % n_pos_blocks))] * 2
        f32_shapes[:2] = [jax.ShapeDtypeStruct((depth, m // seq, GRP, seq), F32)] * 2
    n_in_specs = 11
    return pl.pallas_call(
        kern,
        grid=(m // tm,),
        in_specs=[
            pl.BlockSpec((tm, d), row_blk),
            pl.BlockSpec((None, None, rm, d), mod_idx(0)),
            pl.BlockSpec((None, None, rm, d), mod_idx(1)),
            pl.BlockSpec((None, 1, d), per_layer),
            _resident((None, d, n_in), per_layer),
            pl.BlockSpec((tm, LANES), lambda i: (i % n_pos_blocks, 0)),
            pl.BlockSpec((tm, LANES), lambda i: (i % n_pos_blocks, 0)),
            pl.BlockSpec((None, 1, GRP), per_layer),
            pl.BlockSpec((None, 1, GRP), per_layer),
            pl.BlockSpec((None, 1, GRP), per_layer),
            _resident((GRP, GRP), lambda i: (0, 0)),
        ] + [pl.BlockSpec(memory_space=pl.ANY)] * len(kv_all),
        out_specs=[pl.BlockSpec((tm, n_in), row_blk)] + f32_specs,
        out_shape=[jax.ShapeDtypeStruct((m, n_in), p_dtype)] + f32_shapes,
        input_output_aliases={n_in_specs + a: 1 + a for a in range(len(kv_all))},
        scratch_shapes=[pltpu.VMEM((tm, d), BF16)],
        compiler_params=_cparams(("parallel",)),
        name="inproj",
    )(x2d, mod4, mod4, g_norm, w_in_b, cos_t, sin_t, g_qa, g_ka, g_vb, gsum, *kv_all)


def _softplus2(z2):
    m = jnp.maximum(z2, 0.0)
    lo = z2 - m
    l2 = jnp.log2(1.0 + jnp.exp2(lo - m))
    return lo - l2, m + l2


def _sb_weights(z2, u, c):
    n = z2.shape[1]
    log_sig, sp = _softplus2(z2)
    sums = jnp.dot(sp.astype(BF16), u, preferred_element_type=F32)
    later = sums + jnp.concatenate([c] * (n // LANES), axis=1)
    a = jnp.exp2(log_sig - later).astype(BF16)
    return jnp.broadcast_to(sums[:, :1] + sp[:, :1], c.shape), a


def _sb_weights_chunked(z2, u, c):
    rows, n = z2.shape
    w = u.shape[0]
    n_chunks = n // w
    log_sig, sp = _softplus2(z2)
    stacked = jnp.concatenate([sp[:, j * w:(j + 1) * w] for j in range(n_chunks)], axis=0)
    sums = jnp.dot(stacked.astype(BF16), u, preferred_element_type=F32)
    later = [None] * n_chunks
    for j in reversed(range(n_chunks)):
        blk = sums[j * rows:(j + 1) * rows]
        later[j] = blk + jnp.concatenate([c] * (w // LANES), axis=1)
        c = c + jnp.broadcast_to(blk[:, :1] + sp[:, j * w:j * w + 1], c.shape)
    a = jnp.exp2(log_sig - jnp.concatenate(later, axis=1)).astype(BF16)
    return c, a


def _sb_attn_kernel(aug_ref, dmask_ref, q_ref, k_ref, v_ref, u_ref, o_ref, z_scr, acc_scr, c_scr, *, tq, n_heads, hd,
                    pairs_per_body):
    i = pl.program_id(1)
    heads_per_vreg = LANES // hd
    assert heads_per_vreg == 2
    n_pairs = n_heads // heads_per_vreg
    lane = lax.broadcasted_iota(jnp.int32, (1, LANES), 1)
    u = u_ref[...]
    in_head = [(lane >= sub * hd) & (lane < (sub + 1) * hd) for sub in range(heads_per_vreg)]
    bias_lanes = [(lane >= (1 - sub) * hd) & (lane < (1 - sub) * hd + 2) for sub in range(heads_per_vreg)]

    for grp in range(n_pairs // pairs_per_body):
        pairs = range(grp * pairs_per_body, (grp + 1) * pairs_per_body)
        qh = {}
        for pair in pairs:
            q128 = q_ref[:, pair * LANES:(pair + 1) * LANES]
            for sub in range(heads_per_vreg):
                h = pair * heads_per_vreg + sub
                qh[pair, sub] = jnp.where(in_head[sub], q128, jnp.broadcast_to(aug_ref[h], q128.shape))
                acc_scr[h] = jnp.zeros((tq, LANES), F32)
                c_scr[h] = jnp.zeros((tq, LANES), F32)

        def scores(kb, slot, diagonal):
            start = pl.multiple_of(jnp.maximum(kb, 0) * tq, tq)
            for pair in pairs:
                k128 = k_ref[pl.ds(start, tq), pair * LANES:(pair + 1) * LANES]
                for sub in range(heads_per_vreg):
                    h = pair * heads_per_vreg + sub
                    z = _nt_dot(qh[pair, sub], jnp.where(bias_lanes[sub], jnp.ones_like(k128), k128))
                    z_scr[h, slot] = z + dmask_ref[...] if diagonal else z

        def weigh(kb, slot):
            start = pl.multiple_of(kb * tq, tq)
            for pair in pairs:
                v128 = v_ref[pl.ds(start, tq), pair * LANES:(pair + 1) * LANES]
                for sub in range(heads_per_vreg):
                    h = pair * heads_per_vreg + sub
                    c = c_scr[h]
                    rs, a = _sb_weights(z_scr[h, slot], u, c)
                    acc_scr[h] += jnp.dot(a, v128, preferred_element_type=F32)
                    c_scr[h] = c + rs

        scores(i, 0, True)

        def body(m, carry):
            kb = i - 2 * m
            scores(kb - 1, 1, False)
            weigh(kb, 0)
            scores(kb - 2, 0, False)
            weigh(kb - 1, 1)
            return carry

        n_blocks = i + 1
        lax.fori_loop(0, n_blocks // 2, body, 0)

        @pl.when(n_blocks % 2 == 1)
        def _():
            weigh(0, 0)

        for pair in pairs:
            out128 = acc_scr[pair * heads_per_vreg]
            for sub in range(1, heads_per_vreg):
                out128 = jnp.where(in_head[sub], acc_scr[pair * heads_per_vreg + sub], out128)
            o_ref[:, pair * LANES:(pair + 1) * LANES] = out128.astype(o_ref.dtype)


def _sb_attn(p, aug, dmask, u, *, batch, seq, tq, n_heads, hd, pairs_per_body=1):
    nq = seq // tq
    width = n_heads * hd
    assert width == GRP
    kern = functools.partial(_sb_attn_kernel, tq=tq, n_heads=n_heads, hd=hd, pairs_per_body=pairs_per_body)
    return pl.pallas_call(
        kern,
        grid=(batch, nq),
        in_specs=[
            pl.BlockSpec((n_heads, 1, LANES), lambda b, i: (0, 0, 0)),
            pl.BlockSpec((tq, tq), lambda b, i: (0, 0)),
            pl.BlockSpec((tq, GRP), lambda b, i: (b * nq + i, J_QA)),
            pl.BlockSpec((seq, GRP), lambda b, i: (b, J_KA)),
            pl.BlockSpec((seq, GRP), lambda b, i: (b, J_VA)),
            pl.BlockSpec((tq, tq), lambda b, i: (0, 0)),
        ],
        out_specs=pl.BlockSpec((tq, GRP), lambda b, i: (b * nq + i, 0)),
        out_shape=jax.ShapeDtypeStruct((batch * seq, GRP), BF16),
        scratch_shapes=[pltpu.VMEM((n_heads, 2, tq, tq), F32), pltpu.VMEM((n_heads, tq, LANES), F32),
                        pltpu.VMEM((n_heads, tq, LANES), F32)],
        compiler_params=_cparams(("parallel", "arbitrary")),
        name="sb_attn",
    )(aug, dmask, p, p, p, u)


def _paged_attn_kernel(pt_ref, qbd_ref, bias_ref, mbnew_ref, knew_ref, vnew_ref, u_ref, unew_ref, ck_hbm, cv_hbm,
                       o_ref, kbuf, vbuf, sem, *, layer, n_batch, n_groups, g_pages, page, n_heads, hd, n_tok):
    b = pl.program_id(0)
    n_pages = n_groups * g_pages
    width = n_heads * hd
    n_slots = kbuf.shape[0]
    total = n_batch * n_groups

    def copies(g, slot):
        bb = g // n_groups
        gi = g % n_groups
        out = []
        for pi in range(g_pages):
            pg = pt_ref[bb, n_pages - (gi + 1) * g_pages + pi]
            out.append((pltpu.make_async_copy(ck_hbm.at[layer, pg], kbuf.at[slot, pi], sem.at[0, slot]),
                        pltpu.make_async_copy(cv_hbm.at[layer, pg], vbuf.at[slot, pi], sem.at[1, slot])))
        return out

    def start(g):
        for ck, cv in copies(g, g % n_slots):
            ck.start(priority=0)
            cv.start(priority=1)

    @pl.when(b == 0)
    def _():
        for g in range(min(n_slots - 1, total)):
            start(g)

    qbd = qbd_ref[...]
    bias = bias_ref[...]
    rows = qbd.shape[0]

    rs, a = _sb_weights(jnp.dot(qbd, knew_ref[...], preferred_element_type=F32) + mbnew_ref[...], unew_ref[...],
                        jnp.zeros((rows, LANES), F32))
    c = rs
    acc = _nt_dot(a, vnew_ref[...])

    def body(gi, carry):
        g = b * n_groups + gi
        slot = g % n_slots

        @pl.when(g + n_slots - 1 < total)
        def _():
            start(g + n_slots - 1)

        for ck, cv in copies(g, slot):
            ck.wait()
            cv.wait()
        c, acc = carry
        z2 = jnp.concatenate(
            [jnp.dot(qbd, kbuf[slot, pi].reshape(width, page).astype(BF16), preferred_element_type=F32)
             for pi in range(g_pages)], axis=1) + bias
        c, a = _sb_weights_chunked(z2, u_ref[...], c)
        for pi in range(g_pages):
            acc = acc + _nt_dot(a[:, pi * page:(pi + 1) * page], vbuf[slot, pi].reshape(width, page).astype(BF16))
        return c, acc

    c, acc = lax.fori_loop(0, n_groups, body, (c, acc))

    rr = lax.broadcasted_iota(jnp.int32, acc.shape, 0) % n_heads
    cc = lax.broadcasted_iota(jnp.int32, acc.shape, 1) // hd
    own = jnp.where(rr == cc, acc, 0.0)
    o_ref[...] = jnp.concatenate(
        [jnp.sum(own[t * n_heads:(t + 1) * n_heads], axis=0, keepdims=True) for t in range(n_tok)], axis=0)


def _paged_attn(page_table, qbd, bias_col, mb_new, knew_t, vnew_t, u, u_new, cache_kt, cache_vt, *, layer,
                g_pages, n_slots, n_tok):
    nb, n_pages = page_table.shape
    _, _, n_heads, hd, page = cache_kt.shape
    width = n_heads * hd
    rows = qbd.shape[1]
    assert n_pages % g_pages == 0 and rows == n_tok * n_heads and (g_pages * page) % u.shape[0] == 0
    kern = functools.partial(_paged_attn_kernel, layer=layer, n_batch=nb, n_groups=n_pages // g_pages, g_pages=g_pages,
                             page=page, n_heads=n_heads, hd=hd, n_tok=n_tok)
    const2 = lambda b, pt: (0, 0)
    per_b = lambda b, pt: (b, 0, 0)
    grid_spec = pltpu.PrefetchScalarGridSpec(
        num_scalar_prefetch=1,
        grid=(nb,),
        in_specs=[
            pl.BlockSpec((None, rows, width), per_b),
            pl.BlockSpec((rows, 1), const2),
            pl.BlockSpec((rows, page), const2),
            pl.BlockSpec((None, width, page), per_b),
            pl.BlockSpec((None, width, page), per_b),
            pl.BlockSpec(u.shape, const2),
            pl.BlockSpec((page, page), const2),
            pl.BlockSpec(memory_space=pl.ANY),
            pl.BlockSpec(memory_space=pl.ANY),
        ],
        out_specs=pl.BlockSpec((None, n_tok, width), per_b),
        scratch_shapes=[
            pltpu.VMEM((n_slots, g_pages, n_heads, hd, page), F32),
            pltpu.VMEM((n_slots, g_pages, n_heads, hd, page), F32),
            pltpu.SemaphoreType.DMA((2, n_slots)),
        ],
    )
    return pl.pallas_call(
        kern,
        grid_spec=grid_spec,
        out_shape=jax.ShapeDtypeStruct((nb, n_tok, width), F32),
        compiler_params=_cparams(("arbitrary",)),
        name="paged_attn",
    )(page_table, qbd, bias_col, mb_new, knew_t, vnew_t, u, u_new, cache_kt, cache_vt)


def _mix_prompt_kernel(gpow_ref, ub_ref, vb_ref, qc_ref, kc_ref, vc_ref, gc_ref, wsp_ref, bsp_ref, decay_ref,
                       xi_ref, zeta_ref, ob_ref, oc_ref, sfin_ref, state, *, batch, n_grp, n_heads, t):
    ci = pl.program_id(0)

    @pl.when(ci == 0)
    def _():
        state[...] = jnp.zeros_like(state)

    r = lax.broadcasted_iota(jnp.int32, (t, t), 0)
    s = lax.broadcasted_iota(jnp.int32, (t, t), 1)
    tril = s <= r
    for b, j in [(b, j) for b in range(batch) for j in range(ub_ref.shape[1] // t)]:
        rows = slice(j * t, (j + 1) * t)
        for g in range(n_grp):
            lanes = slice(g * LANES, (g + 1) * LANES)
            w = jnp.where(tril, wsp_ref[g], 0.0).astype(BF16)
            mixed = jnp.dot(w, vb_ref[b, rows, lanes], preferred_element_type=F32) + bsp_ref[g]
            ob_ref[b, rows, lanes] = (ub_ref[b, rows, lanes].astype(F32) * mixed).astype(ob_ref.dtype)
        for h in range(n_heads):
            lanes = slice(h * LANES, (h + 1) * LANES)
            q = qc_ref[b, rows, lanes]
            k = kc_ref[b, rows, lanes]
            v = vc_ref[b, rows, lanes]
            st = state[b, h]
            scores = _nt_dot(q, k) * decay_ref[h]
            inner = jnp.dot(scores.astype(BF16), v, preferred_element_type=F32)
            cross = jnp.dot(q, st.astype(BF16), preferred_element_type=F32) * xi_ref[h]
            kz = (k.astype(F32) * zeta_ref[h]).astype(BF16)
            upd = lax.dot_general(kz, v, (((0,), (0,)), ((), ())), preferred_element_type=F32)
            state[b, h] = gpow_ref[h] * st + upd
            on = _center_norm(inner + cross)
            oc_ref[b, rows, lanes] = (on * gc_ref[b, rows, lanes].astype(F32)).astype(oc_ref.dtype)

    @pl.when(ci == pl.num_programs(0) - 1)
    def _():
        sfin_ref[...] = state[...]


def _mix_prompt(p3, gpow, wsp_l, bsp_b, decay, xi_b, zeta_b, *, n_grp, n_heads, chunks_per_step):
    batch, seq, _ = p3.shape
    t = RET_CHUNK
    ts = chunks_per_step * t
    assert seq % ts == 0
    kern = functools.partial(_mix_prompt_kernel, batch=batch, n_grp=n_grp, n_heads=n_heads, t=t)
    col = lambda jj: (lambda c: (0, c, jj))
    const3 = lambda c: (0, 0, 0)
    return pl.pallas_call(
        kern,
        grid=(seq // ts,),
        in_specs=[
            pl.BlockSpec(memory_space=pltpu.SMEM),
            pl.BlockSpec((batch, ts, GRP), col(J_UB)),
            pl.BlockSpec((batch, ts, GRP), col(J_VB)),
            pl.BlockSpec((batch, ts, GRP), col(J_QC)),
            pl.BlockSpec((batch, ts, GRP), col(J_KC)),
            pl.BlockSpec((batch, ts, GRP), col(J_VC)),
            pl.BlockSpec((batch, ts, GRP), col(J_GC)),
            pl.BlockSpec((n_grp, t, t), const3),
            pl.BlockSpec((n_grp, t, LANES), const3),
            pl.BlockSpec((n_heads, t, t), const3),
            pl.BlockSpec((n_heads, t, LANES), const3),
            pl.BlockSpec((n_heads, t, LANES), const3),
        ],
        out_specs=[
            pl.BlockSpec((batch, ts, GRP), lambda c: (0, c, 0)),
            pl.BlockSpec((batch, ts, GRP), lambda c: (0, c, 0)),
            pl.BlockSpec((batch, n_heads, LANES, LANES), lambda c: (0, 0, 0, 0)),
        ],
        out_shape=[
            jax.ShapeDtypeStruct((batch, seq, GRP), BF16),
            jax.ShapeDtypeStruct((batch, seq, GRP), BF16),
            jax.ShapeDtypeStruct((batch, n_heads, LANES, LANES), F32),
        ],
        scratch_shapes=[pltpu.VMEM((batch, n_heads, LANES, LANES), F32)],
        compiler_params=_cparams(("arbitrary",)),
        name="mix_prompt",
    )(gpow, p3, p3, p3, p3, p3, p3, wsp_l, bsp_b, decay, xi_b, zeta_b)


def _mix_sample_kernel(gpow_ref, ub_ref, vb_ref, qc_ref, kc_ref, vc_ref, gc_ref, mg_ref, bg_ref, dmat_ref,
                       xi_ref, zeta_ref, st_ref, ob_ref, oc_ref, snew_ref, *, n_batch, n_tok):
    h = pl.program_id(0)
    rows = n_batch * n_tok
    mixed = jnp.dot(mg_ref[...].astype(BF16), vb_ref[...].astype(BF16), preferred_element_type=F32) + bg_ref[...]
    ob_ref[...] = ub_ref[...] * mixed

    q = qc_ref[...].astype(BF16)
    k = kc_ref[...]
    v = vc_ref[...].astype(BF16)
    scores = _nt_dot(q, k.astype(BF16)) * dmat_ref[...]
    inner = jnp.dot(scores.astype(BF16), v, preferred_element_type=F32)
    kzt = (k * zeta_ref[...]).T
    gp = gpow_ref[h]
    row_b = lax.broadcasted_iota(jnp.int32, (rows, LANES), 0) // n_tok
    col_b = lax.broadcasted_iota(jnp.int32, (LANES, rows), 1) // n_tok

    def body(b, cross):
        st = st_ref[b]
        cr = jnp.dot(q, st.astype(BF16), preferred_element_type=F32)
        cross = jnp.where(row_b == b, cr, cross)
        upd = jnp.dot(jnp.where(col_b == b, kzt, 0.0).astype(BF16), v, preferred_element_type=F32)
        snew_ref[b] = gp * st + upd
        return cross

    cross = lax.fori_loop(0, n_batch, body, jnp.zeros((rows, LANES), F32))
    on = _center_norm(inner + cross * xi_ref[...])
    oc_ref[...] = on * gc_ref[...]


def _mix_sample(ps, state_ret, layer, gpow4, mg, bg, dmat, xi_s, zeta_s, *, n_batch, n_tok, n_heads):
    rows = n_batch * n_tok
    kern = functools.partial(_mix_sample_kernel, n_batch=n_batch, n_tok=n_tok)
    col = lambda jj: (lambda h: (0, jj * (GRP // LANES) + h))
    per_h = lambda h: (h, 0, 0)
    return pl.pallas_call(
        kern,
        grid=(n_heads,),
        in_specs=[
            pl.BlockSpec(memory_space=pltpu.SMEM),
            pl.BlockSpec((rows, LANES), col(J_UB)),
            pl.BlockSpec((rows, LANES), col(J_VB)),
            pl.BlockSpec((rows, LANES), col(J_QC)),
            pl.BlockSpec((rows, LANES), col(J_KC)),
            pl.BlockSpec((rows, LANES), col(J_VC)),
            pl.BlockSpec((rows, LANES), col(J_GC)),
            pl.BlockSpec((None, rows, rows), per_h),
            pl.BlockSpec((None, rows, LANES), per_h),
            pl.BlockSpec((None, rows, rows), per_h),
            pl.BlockSpec((None, rows, LANES), per_h),
            pl.BlockSpec((None, rows, LANES), per_h),
            pl.BlockSpec((None, n_batch, None, LANES, LANES), lambda h: (layer, 0, h, 0, 0)),
        ],
        out_specs=[
            pl.BlockSpec((rows, LANES), lambda h: (0, h)),
            pl.BlockSpec((rows, LANES), lambda h: (0, h)),
            pl.BlockSpec((n_batch, None, LANES, LANES), lambda h: (0, h, 0, 0)),
        ],
        out_shape=[
            jax.ShapeDtypeStruct((rows, GRP), F32),
            jax.ShapeDtypeStruct((rows, GRP), F32),
            jax.ShapeDtypeStruct((n_batch, n_heads, LANES, LANES), F32),
        ],
        compiler_params=_cparams(("parallel",)),
        name="mix_sample",
    )(gpow4, ps, ps, ps, ps, ps, ps, mg, bg, dmat, xi_s, zeta_s, state_ret)


def _merge_mlp_kernel(x_ref, gate1_ref, shift_ref, scale_ref, gate2_ref, g_ref, oa_ref, ob_ref, oc_ref, ga_ref,
                      gb_ref, gcm_ref, wa_ref, wb_ref, wc_ref, wo_ref, w1_ref, w2_ref, o_ref, h_scr, *, tf):
    def br(o_r, w_r, g_r):
        return g_r[...].astype(F32) * jnp.dot(o_r[...].astype(BF16), w_r[...], preferred_element_type=F32)

    merged = br(oa_ref, wa_ref, ga_ref) + br(ob_ref, wb_ref, gb_ref) + br(oc_ref, wc_ref, gcm_ref)
    x1 = x_ref[...] + gate1_ref[...] * jnp.dot(merged.astype(BF16), wo_ref[...], preferred_element_type=F32)
    o_ref[...] = x1
    ms = jnp.mean(x1 * x1, axis=-1, keepdims=True)
    y = x1 * lax.rsqrt(ms + EPS) * g_ref[...]
    h_scr[...] = (y * (1.0 + scale_ref[...]) + shift_ref[...]).astype(BF16)
    acc = None
    for j in range(w1_ref.shape[1] // tf):
        a = jnp.maximum(jnp.dot(h_scr[...], w1_ref[:, j * tf:(j + 1) * tf], preferred_element_type=F32), 0.0)
        t = jnp.dot((a * a).astype(BF16), w2_ref[j * tf:(j + 1) * tf, :], preferred_element_type=F32)
        acc = t if acc is None else acc + t
    o_ref[...] = o_ref[...] + gate2_ref[...] * acc


def _merge_mlp(x2d, mod4, layer, g_norm, p, oa, ob, oc, wa, wb, wc, wo, w1, w2, *, tm, tf, rows_per_mod):
    m, d = x2d.shape
    rm = mod4.shape[2]
    row = lambda i: (i, 0)
    mod_idx = (lambda chunk: (lambda i: (layer, i // rows_per_mod, 0, chunk)))
    wspec = lambda w: _resident((None,) + w.shape[1:], lambda i: (layer, 0, 0))
    return pl.pallas_call(
        functools.partial(_merge_mlp_kernel, tf=tf),
        grid=(m // tm,),
        in_specs=[
            pl.BlockSpec((tm, d), row),
            pl.BlockSpec((None, None, rm, d), mod_idx(2)),
            pl.BlockSpec((None, None, rm, d), mod_idx(3)),
            pl.BlockSpec((None, None, rm, d), mod_idx(4)),
            pl.BlockSpec((None, None, rm, d), mod_idx(5)),
            pl.BlockSpec((None, 1, d), lambda i: (layer, 0, 0)),
            pl.BlockSpec((tm, GRP), row),
            pl.BlockSpec((tm, GRP), row),
            pl.BlockSpec((tm, GRP), row),
            pl.BlockSpec((tm, d), lambda i: (i, J_GA // 2)),
            pl.BlockSpec((tm, d), lambda i: (i, J_GB // 2)),
            pl.BlockSpec((tm, d), lambda i: (i, J_GCM // 2)),
            wspec(wa), wspec(wb), wspec(wc), wspec(wo), wspec(w1), wspec(w2),
        ],
        out_specs=pl.BlockSpec((tm, d), row),
        out_shape=jax.ShapeDtypeStruct((m, d), F32),
        scratch_shapes=[pltpu.VMEM((tm, d), BF16)],
        compiler_params=_cparams(("parallel",)),
        name="merge_mlp",
    )(x2d, mod4, mod4, mod4, mod4, g_norm, oa, ob, oc, p, p, p, wa, wb, wc, wo, w1, w2)


def _rope_tables(pos, dk):
    half = dk // 2
    freqs = ROPE_BASE ** (-jnp.arange(half, dtype=F32) / half)
    ang = pos.astype(F32)[:, None] * freqs[None, :]
    cos, sin = jnp.cos(ang), jnp.sin(ang)
    return jnp.concatenate([cos, cos], axis=1), jnp.concatenate([-sin, sin], axis=1)


def kernel(x_prompt, x_sample, cache_k, cache_v, state_ret, page_table, c_prompt, c_sample, w_ada, b_ada, g_norm_mix, g_norm_mlp, w_in, g_qa, g_ka, sb_bias, g_vb, w_spatial, b_spatial, w_proj_a, w_proj_b, w_proj_c, w_out, w_mlp1, w_mlp2):
    batch, seq, d = x_prompt.shape
    nb, n_tok, _ = x_sample.shape
    depth, n_pool, page, n_heads_a, hd = cache_k.shape
    n_pages = page_table.shape[1]
    past = n_pages * page
    n_grp_b, chunk_b = w_spatial.shape[1], w_spatial.shape[2]
    w_b = g_vb.shape[1]
    n_heads_c, dk, dv = state_ret.shape[2:]
    w_a = n_heads_a * hd
    w_c = n_heads_c * dv
    rows_s = nb * n_tok
    assert w_a == GRP and w_b == GRP and w_c == GRP and n_heads_c * dk == GRP and d == 2 * GRP
    assert dk == LANES and dv == LANES and w_b // n_grp_b == LANES and chunk_b == RET_CHUNK and n_grp_b == n_heads_c
    assert seq % RET_CHUNK == 0 and n_tok <= chunk_b

    w_in_b = w_in.astype(BF16)
    wa_b, wb_b, wc_b, wo_b = (w.astype(BF16) for w in (w_proj_a, w_proj_b, w_proj_c, w_out))
    w1_b, w2_b = w_mlp1.astype(BF16), w_mlp2.astype(BF16)
    g_mix = g_norm_mix.reshape(depth, 1, d)
    g_mlp = g_norm_mlp.reshape(depth, 1, d)
    g_qa_t = jnp.tile(g_qa, (1, n_heads_a)).reshape(depth, 1, GRP)
    g_ka_t = jnp.tile(g_ka, (1, n_heads_a)).reshape(depth, 1, GRP)
    g_vb_r = g_vb.reshape(depth, 1, GRP)
    gi = np.arange(GRP) // hd
    gsum = jnp.asarray((gi[:, None] == gi[None, :]).astype(np.float32) / hd, dtype=BF16)

    log_gamma = jnp.log1p(-jnp.exp2(-5.0 - jnp.arange(n_heads_c, dtype=F32)))

    def ret_tables(length):
        i = jnp.arange(length, dtype=F32)
        diff = i[:, None] - i[None, :]
        decay = jnp.where(diff >= 0, jnp.exp(log_gamma[:, None, None] * jnp.maximum(diff, 0.0)), 0.0)
        xi = jnp.exp(log_gamma[:, None] * (i[None, :] + 1.0))
        zeta = jnp.exp(log_gamma[:, None] * (length - 1.0 - i[None, :]))
        return decay, xi, zeta, jnp.exp(log_gamma * length)

    decay_p, xi_p, zeta_p, gpow_p = ret_tables(RET_CHUNK)
    xi_pb = jnp.broadcast_to(xi_p[:, :, None], (n_heads_c, RET_CHUNK, LANES))
    zeta_pb = jnp.broadcast_to(zeta_p[:, :, None], (n_heads_c, RET_CHUNK, LANES))
    decay_s, xi_s, zeta_s, gpow_s = ret_tables(n_tok)
    eye_b = jnp.eye(nb, dtype=F32)
    dmat_s = jnp.stack([jnp.kron(eye_b, decay_s[h]) for h in range(n_heads_c)])
    xi_sb = jnp.broadcast_to(jnp.tile(xi_s, (1, nb))[:, :, None], (n_heads_c, rows_s, LANES))
    zeta_sb = jnp.broadcast_to(jnp.tile(zeta_s, (1, nb))[:, :, None], (n_heads_c, rows_s, LANES))

    bsp_pb = jnp.broadcast_to(b_spatial[:, :, :, None], (depth, n_grp_b, chunk_b, LANES))
    w_sp_s = jnp.tril(w_spatial[:, :, :n_tok, :n_tok])
    mg_s = jnp.einsum("ab,lgts->lgatbs", eye_b, w_sp_s).reshape(depth, n_grp_b, rows_s, rows_s)
    bg_s = jnp.broadcast_to(jnp.tile(b_spatial[:, :, :n_tok], (1, 1, nb))[..., None],
                            (depth, n_grp_b, rows_s, LANES))

    cos_p, sin_p = _rope_tables(jnp.arange(seq), dk)
    cos_s, sin_s = _rope_tables(past + (jnp.arange(rows_s) % n_tok), dk)

    def later_sum_matrix(n):
        tri = np.arange(n)
        return jnp.asarray((tri[:, None] > tri[None, :]).astype(np.float32), dtype=BF16)

    tq = 256
    g_pages = 8
    assert n_tok <= page
    u_p = later_sum_matrix(tq)
    u_s = later_sum_matrix(512)
    u_new = later_sum_matrix(page)

    m_all = batch + nb
    m_pad = -(-m_all // 8) * 8
    c_all = jnp.concatenate([c_prompt, c_sample, jnp.zeros((m_pad - m_all, d), F32)], axis=0)
    mod = _adaln(c_all, w_ada, b_ada)
    mod_p = mod[:, :batch].reshape(depth, batch, 1, 6 * d)
    mod_s = jnp.repeat(mod[:, batch:m_all], n_tok, axis=1).reshape(depth, 1, rows_s, 6 * d)

    sb_bias2 = sb_bias * LOG2E
    bias_hi = sb_bias2.astype(BF16)
    bias_lo = (sb_bias2 - bias_hi.astype(F32)).astype(BF16)
    first_bias_lane = (1 - np.arange(n_heads_a) % 2) * hd
    lane_idx = np.arange(LANES)[None, :]
    aug_p = (jnp.where(jnp.asarray(lane_idx == first_bias_lane[:, None])[None], bias_hi[:, :, None], 0)
             + jnp.where(jnp.asarray(lane_idx == first_bias_lane[:, None] + 1)[None], bias_lo[:, :, None], 0)
             ).astype(BF16).reshape(depth, n_heads_a, 1, LANES)
    dmask_p = jnp.where(jnp.asarray(np.arange(tq)[None, :] < np.arange(tq)[:, None]), 0.0, -jnp.inf).astype(F32)
    q_head = np.arange(n_tok * n_heads_a) % n_heads_a
    q_tok = np.arange(n_tok * n_heads_a) // n_heads_a
    row_bias = sb_bias2[:, q_head]
    bias_col = row_bias[:, :, None]
    visible_new = jnp.asarray(np.arange(page)[None, :] < q_tok[:, None])
    mb_new = jnp.where(visible_new[None], bias_col, -jnp.inf)
    qbd_mask = jnp.asarray(q_head[:, None] == (np.arange(w_a) // hd)[None, :])
    cache_kt = jnp.transpose(cache_k, (0, 1, 3, 4, 2))
    cache_vt = jnp.transpose(cache_v, (0, 1, 3, 4, 2))

    tm_p = 512
    xp = x_prompt.reshape(batch * seq, d)
    xs = x_sample.reshape(rows_s, d)
    ks_l, vs_l, sp_l, ss_l, gv_l = [], [], [], [], []
    kp_all = jnp.zeros((depth, batch, w_a, seq), F32)
    vp_all = jnp.zeros((depth, batch, w_a, seq), F32)
    for l in range(depth):
        p, kp_all, vp_all = _inproj(xp, mod_p, l, g_mix, w_in_b, cos_p, sin_p, g_qa_t, g_ka_t, g_vb_r, gsum,
                                    tm=tm_p, rows_per_mod=seq // tm_p, p_dtype=BF16, hd=hd, dk=dk, with_vb=False,
                                    kv_transposed=True, kv_all=(kp_all, vp_all))
        oa = _sb_attn(p, aug_p[l], dmask_p, u_p, batch=batch, seq=seq, tq=tq, n_heads=n_heads_a, hd=hd, pairs_per_body=4)
        ob, oc, s_fin = _mix_prompt(p.reshape(batch, seq, -1), gpow_p, w_spatial[l], bsp_pb[l], decay_p, xi_pb,
                                    zeta_pb, n_grp=n_grp_b, n_heads=n_heads_c, chunks_per_step=4)
        xp = _merge_mlp(xp, mod_p, l, g_mlp, p, oa, ob.reshape(batch * seq, GRP), oc.reshape(batch * seq, GRP),
                        wa_b, wb_b, wc_b, wo_b, w1_b, w2_b, tm=tm_p, tf=1024, rows_per_mod=seq // tm_p)
        sp_l.append(s_fin)

        ps, ks, vs, vbs = _inproj(xs, mod_s, l, g_mix, w_in_b, cos_s, sin_s, g_qa_t, g_ka_t, g_vb_r, gsum,
                                  tm=rows_s, rows_per_mod=1, p_dtype=F32, hd=hd, dk=dk, with_vb=True,
                                  kv_transposed=False)
        q_s = ps[:, J_QA * GRP:(J_QA + 1) * GRP].reshape(nb, n_tok, 1, w_a)
        qbd = jnp.where(qbd_mask, jnp.broadcast_to(q_s, (nb, n_tok, n_heads_a, w_a)).reshape(nb, -1, w_a), 0.0)
        pad = ((0, 0), (0, 0), (0, page - n_tok))
        knew_t = jnp.pad(jnp.transpose(ks.reshape(nb, n_tok, w_a), (0, 2, 1)), pad).astype(BF16)
        vnew_t = jnp.pad(jnp.transpose(vs.reshape(nb, n_tok, w_a), (0, 2, 1)), pad).astype(BF16)
        oas = _paged_attn(page_table, qbd.astype(BF16), bias_col[l], mb_new[l], knew_t, vnew_t, u_s, u_new,
                          cache_kt, cache_vt, layer=l, g_pages=g_pages, n_slots=4, n_tok=n_tok)
        obs, ocs, s_new = _mix_sample(ps, state_ret, l, gpow_s, mg_s[l], bg_s[l], dmat_s, xi_sb, zeta_sb,
                                      n_batch=nb, n_tok=n_tok, n_heads=n_heads_c)
        xs = _merge_mlp(xs, mod_s, l, g_mlp, ps, oas.reshape(rows_s, w_a), obs, ocs, wa_b, wb_b, wc_b, wo_b,
                        w1_b, w2_b, tm=rows_s, tf=1024, rows_per_mod=1)
        ks_l.append(ks)
        vs_l.append(vs)
        ss_l.append(s_new)
        gv_l.append(vbs)

    def kv(lst, b_, l_):
        return jnp.stack(lst).reshape(depth, b_, l_, n_heads_a, hd)

    def kv_t(buf):
        return jnp.transpose(buf.reshape(depth, batch, n_heads_a, hd, seq), (0, 1, 4, 2, 3))

    return (xp.reshape(batch, seq, d), xs.reshape(nb, n_tok, d),
            kv_t(kp_all), kv_t(vp_all), kv(ks_l, nb, n_tok), kv(vs_l, nb, n_tok),
            jnp.stack(sp_l), jnp.stack(ss_l), jnp.stack(gv_l).reshape(depth, nb, n_tok, w_b))
```
